```python
import jax, jax.numpy as jnp
from jax import lax
import numpy as np

D_MODEL = 1024
BATCH = 16
SEQ = 4096
DEPTH = 4

N_MIXERS = 2
PLE_DIM = 256
CONV_DIM = D_MODEL
CONV_WIDTH = 31
HEAD_DIM = 64
HEADS_PER_GROUP = 8
DILATION_GROUPS = ((128, 1), (512, 4), (2048, 16))
N_GROUPS = len(DILATION_GROUPS)
QKV_DIM = N_GROUPS * HEADS_PER_GROUP * HEAD_DIM
ATTN_OUT_DIM = HEADS_PER_GROUP * HEAD_DIM
BLOCK = 128
ROPE_THETA = 500000.0
ROT_DIM = HEAD_DIM // 4
EPS = 1e-6
NEG_INF = -1e30
N_CONV_LAYERS = (DEPTH + 1) // 2
N_ATTN_LAYERS = DEPTH // 2
OUT_SCALE = (2.0 * DEPTH) ** -0.5
PLE_SCALE = 0.5

kernel_name = "hybrid_conv_dilated_attn_ple"


def rms_norm(x, g):
    x32 = x.astype(jnp.float32)
    y = x32 * lax.rsqrt(jnp.mean(x32 * x32, axis=-1, keepdims=True) + EPS)
    return (y * g.astype(jnp.float32)).astype(x.dtype)


def layer_norm(x, g, b):
    x32 = x.astype(jnp.float32)
    xc = x32 - jnp.mean(x32, axis=-1, keepdims=True)
    y = xc * lax.rsqrt(jnp.mean(xc * xc, axis=-1, keepdims=True) + EPS)
    return (y * g.astype(jnp.float32) + b.astype(jnp.float32)).astype(x.dtype)


def rope_tables(positions):
    inv_freq = 1.0 / (ROPE_THETA ** (jnp.arange(0, ROT_DIM, 2, dtype=jnp.float32) / ROT_DIM))
    ang = positions.astype(jnp.float32)[..., None] * inv_freq
    return jnp.cos(ang), jnp.sin(ang)


def partial_rope(x, cos, sin):
    c = cos[:, :, None, None, :]
    s = sin[:, :, None, None, :]
    half = ROT_DIM // 2
    xr = x[..., :ROT_DIM].astype(jnp.float32)
    x1, x2 = xr[..., :half], xr[..., half:]
    rot = jnp.concatenate([x1 * c - x2 * s, x2 * c + x1 * s], axis=-1).astype(x.dtype)
    return jnp.concatenate([rot, x[..., ROT_DIM:]], axis=-1)


def dilated_band_attention(q, k, v, dilation, band):
    B, S, H, hd = q.shape
    L = S // dilation
    nb = -(-L // BLOCK)
    Lp = nb * BLOCK

    def to_blocks(t):
        t = t.reshape(B, L, dilation, H, hd).transpose(0, 2, 1, 3, 4)
        t = jnp.pad(t, ((0, 0), (0, 0), (0, Lp - L), (0, 0), (0, 0)))
        return t.reshape(B, dilation, nb, BLOCK, H, hd)

    def with_prev(t):
        prev = jnp.pad(t, ((0, 0), (0, 0), (1, 0), (0, 0), (0, 0), (0, 0)))[:, :, :-1]
        return jnp.concatenate([prev, t], axis=3)

    qb, kb, vb = to_blocks(q), to_blocks(k), to_blocks(v)
    kk, vv = with_prev(kb), with_prev(vb)
    s = jnp.einsum('bdnqhc,bdnkhc->bdnhqk', qb, kk,
                   preferred_element_type=jnp.float32) * (hd ** -0.5)
    qi = jnp.arange(BLOCK)[:, None]
    kj = jnp.arange(2 * BLOCK)[None, :]
    dist = BLOCK + qi - kj
    blk = jnp.arange(nb)[:, None, None]
    valid = (dist >= 0) & (dist <= band) & (blk * BLOCK + kj - BLOCK >= 0)
    s = jnp.where(valid[:, None], s, NEG_INF)
    m = jnp.max(s, axis=-1, keepdims=True)
    e = jnp.exp(s - m)
    den = jnp.sum(e, axis=-1, keepdims=True)
    probs = (e / den).astype(v.dtype)
    o = jnp.einsum('bdnhqk,bdnkhc->bdnqhc', probs, vv)
    lse = (m + jnp.log(den))[..., 0]
    o = o.reshape(B, dilation, Lp, H, hd)[:, :, :L].transpose(0, 2, 1, 3, 4).reshape(B, S, H, hd)
    lse = lse.transpose(0, 1, 2, 4, 3).reshape(B, dilation, Lp, H)[:, :, :L]
    lse = lse.transpose(0, 2, 1, 3).reshape(B, S, H)
    return o, lse


def conformer_conv_branch(h, w_in, dw, dw_b, ln_g, ln_b, w_out):
    u = h @ w_in
    a, b, gate = jnp.split(u, [CONV_DIM, 2 * CONV_DIM], axis=-1)
    y = a * jax.nn.sigmoid(b)
    y = lax.conv_general_dilated(
        y, dw[:, None, :].astype(y.dtype), window_strides=(1,),
        padding=((CONV_WIDTH - 1, 0),), dimension_numbers=('NWC', 'WIO', 'NWC'),
        feature_group_count=CONV_DIM) + dw_b
    y = layer_norm(y, ln_g, ln_b)
    y = jax.nn.silu(y) * jax.nn.silu(gate)
    return y @ w_out


def dilated_attention_branch(h, cos, sin, w_in, q_norm, k_norm, w_out):
    B, S, _ = h.shape
    u = h @ w_in
    q, k, v, gate = jnp.split(u, [QKV_DIM, 2 * QKV_DIM, 3 * QKV_DIM], axis=-1)
    shp = (B, S, N_GROUPS, HEADS_PER_GROUP, HEAD_DIM)
    q = partial_rope(rms_norm(q.reshape(shp), q_norm), cos, sin)
    k = partial_rope(rms_norm(k.reshape(shp), k_norm), cos, sin)
    v = v.reshape(shp)
    outs, lses = [], []
    for g, (window, dilation) in enumerate(DILATION_GROUPS):
        o, lse = dilated_band_attention(q[:, :, g], k[:, :, g], v[:, :, g], dilation, window // dilation)
        outs.append(o)
        lses.append(lse)
    wts = jax.nn.softmax(jnp.stack(lses, axis=0), axis=0)
    o = jnp.sum(wts[..., None] * jnp.stack(outs, axis=0).astype(jnp.float32), axis=0).astype(h.dtype)
    y = o.reshape(B, S, ATTN_OUT_DIM) * jax.nn.silu(gate)
    return y @ w_out


def setup_inputs(seed: int = 0) -> dict:
    key = jax.random.key(seed)
    ks = jax.random.split(key, 20)
    f32 = jnp.float32
    nrm = lambda k, shape: jax.random.normal(k, shape, dtype=f32)
    x = nrm(ks[0], (BATCH, SEQ, D_MODEL))
    p = nrm(ks[1], (DEPTH, BATCH, SEQ, PLE_DIM))
    offsets = jax.random.randint(ks[2], (BATCH, 1), 0, 1024, dtype=jnp.int32)
    positions = (jnp.arange(SEQ, dtype=jnp.int32)[None, :] + offsets).astype(jnp.int32)
    norm_g = 1.0 + 0.02 * nrm(ks[3], (DEPTH, D_MODEL))
    conv_w_in = nrm(ks[4], (N_CONV_LAYERS, D_MODEL, 3 * CONV_DIM)) * D_MODEL ** -0.5
    conv_dw = nrm(ks[5], (N_CONV_LAYERS, CONV_WIDTH, CONV_DIM)) * CONV_WIDTH ** -0.5
    conv_dw_b = 0.02 * nrm(ks[6], (N_CONV_LAYERS, CONV_DIM))
    conv_ln_g = 1.0 + 0.02 * nrm(ks[7], (N_CONV_LAYERS, CONV_DIM))
    conv_ln_b = 0.02 * nrm(ks[8], (N_CONV_LAYERS, CONV_DIM))
    conv_w_out = nrm(ks[9], (N_CONV_LAYERS, CONV_DIM, D_MODEL)) * (CONV_DIM ** -0.5 * OUT_SCALE)
    attn_w_in = nrm(ks[10], (N_ATTN_LAYERS, D_MODEL, 3 * QKV_DIM + ATTN_OUT_DIM)) * D_MODEL ** -0.5
    attn_q_norm = 1.0 + 0.02 * nrm(ks[11], (N_ATTN_LAYERS, HEAD_DIM))
    attn_k_norm = 1.0 + 0.02 * nrm(ks[12], (N_ATTN_LAYERS, HEAD_DIM))
    attn_w_out = nrm(ks[13], (N_ATTN_LAYERS, ATTN_OUT_DIM, D_MODEL)) * (ATTN_OUT_DIM ** -0.5 * OUT_SCALE)
    ple_w_proj = nrm(ks[14], (DEPTH, PLE_DIM, D_MODEL)) * (PLE_DIM ** -0.5 * PLE_SCALE)
    ple_norm_g = 1.0 + 0.02 * nrm(ks[15], (DEPTH, D_MODEL))
    ple_w_gate = nrm(ks[16], (DEPTH, D_MODEL, D_MODEL)) * D_MODEL ** -0.5
    return {'x': x, 'p': p, 'positions': positions, 'norm_g': norm_g,
            'conv_w_in': conv_w_in, 'conv_dw': conv_dw, 'conv_dw_b': conv_dw_b,
            'conv_ln_g': conv_ln_g, 'conv_ln_b': conv_ln_b, 'conv_w_out': conv_w_out,
            'attn_w_in': attn_w_in, 'attn_q_norm': attn_q_norm, 'attn_k_norm': attn_k_norm,
            'attn_w_out': attn_w_out, 'ple_w_proj': ple_w_proj, 'ple_norm_g': ple_norm_g,
            'ple_w_gate': ple_w_gate}


def reference(x, p, positions, norm_g, conv_w_in, conv_dw, conv_dw_b, conv_ln_g, conv_ln_b,
              conv_w_out, attn_w_in, attn_q_norm, attn_k_norm, attn_w_out, ple_w_proj,
              ple_norm_g, ple_w_gate):
    cos, sin = rope_tables(positions)
    for i in range(DEPTH):
        h = rms_norm(x, norm_g[i])
        j = i // N_MIXERS
        if i % N_MIXERS == 0:
            x = x + conformer_conv_branch(h, conv_w_in[j], conv_dw[j], conv_dw_b[j],
                                          conv_ln_g[j], conv_ln_b[j], conv_w_out[j])
        else:
            x = x + dilated_attention_branch(h, cos, sin, attn_w_in[j], attn_q_norm[j],
                                             attn_k_norm[j], attn_w_out[j])
        gate = jax.nn.sigmoid(rms_norm(x, ple_norm_g[i]) @ ple_w_gate[i])
        x = x + (p[i] @ ple_w_proj[i]) * gate
    return x
```

```python
import functools

import jax
import jax.numpy as jnp
from jax import lax
from jax.experimental import pallas as pl
from jax.experimental.pallas import tpu as pltpu

F32 = jnp.float32
BF16 = jnp.bfloat16

D_MODEL = 1024
PLE_DIM = 256
CONV_DIM = D_MODEL
CONV_WIDTH = 31
HEAD_DIM = 64
HEADS_PER_GROUP = 8
DILATION_GROUPS = ((128, 1), (512, 4), (2048, 16))
N_GROUPS = len(DILATION_GROUPS)
GROUP_DIM = HEADS_PER_GROUP * HEAD_DIM
QKV_DIM = N_GROUPS * GROUP_DIM
BAND = 128
ROPE_THETA = 500000.0
ROT_DIM = HEAD_DIM // 4
EPS = 1e-6
NEG_INF = -1e30

LANES = 128
CONV_HALO = 32
VMEM_LIMIT = 56 * 1024 * 1024

SEQ_TILE = 512
CONV_ROWS = 128
Q_BLOCK = 128


def _const_spec(shape):
    return pl.BlockSpec(shape, lambda *_: (0,) * len(shape), pipeline_mode=pl.Buffered(1))


def _dot(a, b):
    return jnp.dot(a, b, preferred_element_type=F32)


def _rms_norm(x, g):
    return x * lax.rsqrt(jnp.mean(x * x, axis=-1, keepdims=True) + EPS) * g


def _silu(x):
    return x * jax.nn.sigmoid(x)


def _ple_update(x1, p, png, w_gate, w_proj):
    gate = jax.nn.sigmoid(_dot(_rms_norm(x1, png).astype(BF16), w_gate))
    return x1 + _dot(p.astype(BF16), w_proj) * gate


def _rope_table_kernel(pos_ref, freq_ref, sign_ref, cos_ref, sin_ref):
    ang = pos_ref[0].astype(F32) * freq_ref[...]
    cos_ref[0] = jnp.cos(ang)
    sin_ref[0] = jnp.sin(ang) * sign_ref[...]


def _rope_tables(positions):
    B, S = positions.shape
    lane = jnp.arange(LANES) % HEAD_DIM
    half = ROT_DIM // 2
    inv_freq = 1.0 / (ROPE_THETA ** (jnp.arange(0, ROT_DIM, 2, dtype=F32) / ROT_DIM))
    freq = jnp.where(lane < ROT_DIM, inv_freq[lane % half], 0.0).astype(F32)[None, :]
    sign = jnp.where(lane < half, -1.0, jnp.where(lane < ROT_DIM, 1.0, 0.0)).astype(F32)[None, :]
    ts = SEQ_TILE
    return pl.pallas_call(
        _rope_table_kernel,
        grid=(B, S // ts),
        in_specs=[pl.BlockSpec((1, ts, 1), lambda b, s: (b, s, 0)),
                  _const_spec((1, LANES)), _const_spec((1, LANES))],
        out_specs=[pl.BlockSpec((1, ts, LANES), lambda b, s: (b, s, 0))] * 2,
        out_shape=[jax.ShapeDtypeStruct((B, S, LANES), F32)] * 2,
        compiler_params=pltpu.CompilerParams(dimension_semantics=("parallel", "parallel")),
        name="rope_tables",
    )(positions.reshape(B, S, 1), freq, sign)


def _conv_layer_kernel(x_ref, p_ref, ng_ref, win_ref, dw_ref, dwb_ref, lng_ref, lnb_ref, wout_ref,
                       png_ref, wgate_ref, wproj_ref, o_ref, ypad_ref, acc_ref):
    ts = x_ref.shape[1]
    c = CONV_DIM

    @pl.when(pl.program_id(1) == 0)
    def _():
        ypad_ref[0:CONV_HALO, :] = jnp.zeros((CONV_HALO, c), F32)

    x = x_ref[0]
    h = _rms_norm(x, ng_ref[...]).astype(BF16)
    a = _dot(h, win_ref[:, 0:c])
    b = _dot(h, win_ref[:, c:2 * c])
    ypad_ref[CONV_HALO:CONV_HALO + ts, :] = a * jax.nn.sigmoid(b)

    tap0 = CONV_HALO - (CONV_WIDTH - 1)

    for rc in range(ts // CONV_ROWS):
        r0 = rc * CONV_ROWS
        for cb in range(c // LANES):
            cols = slice(cb * LANES, (cb + 1) * LANES)
            acc = jnp.broadcast_to(dwb_ref[:, cols], (CONV_ROWS, LANES))
            for k in range(CONV_WIDTH):
                acc = acc + dw_ref[k:k + 1, cols] * ypad_ref[r0 + tap0 + k:r0 + tap0 + k + CONV_ROWS, cols]
            acc_ref[r0:r0 + CONV_ROWS, cols] = acc
    ypad_ref[0:CONV_HALO, :] = ypad_ref[ts:ts + CONV_HALO, :]

    y = acc_ref[...]
    yc = y - jnp.mean(y, axis=-1, keepdims=True)
    yn = yc * lax.rsqrt(jnp.mean(yc * yc, axis=-1, keepdims=True) + EPS) * lng_ref[...] + lnb_ref[...]
    gate = _dot(h, win_ref[:, 2 * c:3 * c])
    z = (_silu(yn) * _silu(gate)).astype(BF16)
    x1 = x + _dot(z, wout_ref[...])
    o_ref[0] = _ple_update(x1, p_ref[0], png_ref[...], wgate_ref[...], wproj_ref[...])


def _conv_layer(x, p, layer, ng, w_in, dw, dw_b, ln_g, ln_b, w_out, png, w_gate, w_proj):
    B, S, D = x.shape
    ts = SEQ_TILE
    row = lambda v: v.reshape(1, -1).astype(F32)
    return pl.pallas_call(
        _conv_layer_kernel,
        grid=(B, S // ts),
        in_specs=[pl.BlockSpec((1, ts, D), lambda b, s: (b, s, 0)),
                  pl.BlockSpec((None, 1, ts, PLE_DIM), lambda b, s: (layer, b, s, 0)),
                  _const_spec((1, D)), _const_spec((D, 3 * CONV_DIM)),
                  _const_spec((CONV_WIDTH, CONV_DIM)), _const_spec((1, CONV_DIM)),
                  _const_spec((1, CONV_DIM)), _const_spec((1, CONV_DIM)),
                  _const_spec((CONV_DIM, D)), _const_spec((1, D)),
                  _const_spec((D, D)), _const_spec((PLE_DIM, D))],
        out_specs=pl.BlockSpec((1, ts, D), lambda b, s: (b, s, 0)),
        out_shape=jax.ShapeDtypeStruct((B, S, D), F32),
        scratch_shapes=[pltpu.VMEM((CONV_HALO + ts, CONV_DIM), F32),
                        pltpu.VMEM((ts, CONV_DIM), F32)],
        compiler_params=pltpu.CompilerParams(dimension_semantics=("parallel", "arbitrary"),
                                             vmem_limit_bytes=VMEM_LIMIT),
        name="conv_layer",
    )(x, p, row(ng), w_in.astype(BF16), dw.astype(F32), row(dw_b), row(ln_g), row(ln_b),
      w_out.astype(BF16), row(png), w_gate.astype(BF16), w_proj.astype(BF16))


def _attn_in_kernel(x_ref, ng_ref, win_ref, qg_ref, kg_ref, cos_ref, sin_ref, hsum_ref,
                    q0_ref, q1_ref, q2_ref, k0_ref, k1_ref, k2_ref, v0_ref, v1_ref, v2_ref,
                    sg_ref, perm_ref):
    ts = x_ref.shape[1]
    h = _rms_norm(x_ref[0], ng_ref[...]).astype(BF16)
    cos_t = cos_ref[0]
    sin_t = sin_ref[0]
    lane = lax.broadcasted_iota(jnp.int32, (1, LANES), 1) % HEAD_DIM
    first_half = lane < (ROT_DIM // 2)
    outs = ((q0_ref, q1_ref, q2_ref), (k0_ref, k1_ref, k2_ref), (v0_ref, v1_ref, v2_ref))
    gains = (qg_ref, kg_ref)

    def emit(out_ref, g, j, val, slab):
        d = DILATION_GROUPS[g][1]
        cols = slice(j * LANES, (j + 1) * LANES)
        if d == 1:
            out_ref[0, :, cols] = val.astype(BF16)
            return
        perm_ref[slab] = val
        for r in range(d):
            out_ref[0, r, :, cols] = perm_ref[slab, pl.ds(r, ts // d, stride=d), :].astype(BF16)

    for kind in range(3):
        for g in range(N_GROUPS):
            c0 = kind * QKV_DIM + g * GROUP_DIM
            u = _dot(h, win_ref[:, c0:c0 + GROUP_DIM])
            for j in range(GROUP_DIM // LANES):
                ub = u[:, j * LANES:(j + 1) * LANES]
                if kind < 2:
                    ss = _dot((ub * ub).astype(BF16), hsum_ref[...])
                    un = ub * lax.rsqrt(ss * (1.0 / HEAD_DIM) + EPS) * gains[kind][...]
                    partner = jnp.where(first_half, pltpu.roll(un, LANES - ROT_DIM // 2, 1),
                                        pltpu.roll(un, ROT_DIM // 2, 1))
                    ub = un * cos_t + partner * sin_t
                    if kind == 0:
                        ub = ub * (HEAD_DIM ** -0.5)
                emit(outs[kind][g], g, j, ub, j)
    gate = _dot(h, win_ref[:, 3 * QKV_DIM:3 * QKV_DIM + GROUP_DIM])
    sg_ref[0] = _silu(gate).astype(BF16)


def _attn_in(x, ng, w_in, q_gain, k_gain, cos_t, sin_t):
    B, S, D = x.shape
    ts = SEQ_TILE
    hsum = (jnp.arange(LANES)[:, None] // HEAD_DIM == jnp.arange(LANES)[None, :] // HEAD_DIM).astype(BF16)
    lane_gain = lambda v: jnp.tile(v.astype(F32), LANES // HEAD_DIM)[None, :]

    def group_spec(g):
        d = DILATION_GROUPS[g][1]
        if d == 1:
            return pl.BlockSpec((1, ts, GROUP_DIM), lambda b, s: (b, s, 0))
        return pl.BlockSpec((1, d, ts // d, GROUP_DIM), lambda b, s: (b, 0, s, 0))

    def group_shape(g):
        d = DILATION_GROUPS[g][1]
        if d == 1:
            return jax.ShapeDtypeStruct((B, S, GROUP_DIM), BF16)
        return jax.ShapeDtypeStruct((B, d, S // d, GROUP_DIM), BF16)

    w_cols = 3 * QKV_DIM + GROUP_DIM
    return pl.pallas_call(
        _attn_in_kernel,
        grid=(B, S // ts),
        in_specs=[pl.BlockSpec((1, ts, D), lambda b, s: (b, s, 0)),
                  _const_spec((1, D)), _const_spec((D, w_cols)),
                  _const_spec((1, LANES)), _const_spec((1, LANES)),
                  pl.BlockSpec((1, ts, LANES), lambda b, s: (b, s, 0)),
                  pl.BlockSpec((1, ts, LANES), lambda b, s: (b, s, 0)),
                  _const_spec((LANES, LANES))],
        out_specs=[group_spec(g) for _ in range(3) for g in range(N_GROUPS)]
        + [pl.BlockSpec((1, ts, GROUP_DIM), lambda b, s: (b, s, 0))],
        out_shape=[group_shape(g) for _ in range(3) for g in range(N_GROUPS)]
        + [jax.ShapeDtypeStruct((B, S, GROUP_DIM), BF16)],
        scratch_shapes=[pltpu.VMEM((GROUP_DIM // LANES, ts, LANES), F32)],
        compiler_params=pltpu.CompilerParams(dimension_semantics=("parallel", "parallel"),
                                             vmem_limit_bytes=VMEM_LIMIT),
        name="attn_in",
    )(x, ng.reshape(1, -1), w_in.astype(BF16), lane_gain(q_gain), lane_gain(k_gain), cos_t, sin_t, hsum)


def _band_attn_kernel(q_ref, k_ref, v_ref, o_ref, lse_ref, kf_ref, vf_ref):
    tq = q_ref.shape[2]
    i = pl.program_id(2)

    @pl.when(i == 0)
    def _():
        kf_ref[0:BAND, :] = jnp.zeros((BAND, GROUP_DIM), BF16)
        vf_ref[0:BAND, :] = jnp.zeros((BAND, GROUP_DIM), BF16)

    kf_ref[BAND:BAND + tq, :] = k_ref[0, 0]
    vf_ref[BAND:BAND + tq, :] = v_ref[0, 0]

    lane = lax.broadcasted_iota(jnp.int32, (1, LANES), 1)
    low = lane < HEAD_DIM
    head_mask = (low.astype(BF16), (~low).astype(BF16))
    qi = lax.broadcasted_iota(jnp.int32, (Q_BLOCK, 2 * BAND), 0)
    kj = lax.broadcasted_iota(jnp.int32, (Q_BLOCK, 2 * BAND), 1)
    dist = BAND + qi - kj
    band_bias = jnp.where((dist >= 0) & (dist <= BAND), 0.0, NEG_INF).astype(F32)

    def sub_block(sb, carry):
        r0 = pl.multiple_of(sb * Q_BLOCK, Q_BLOCK)
        prev_bias = jnp.where((i * tq + r0) > 0, 0.0, NEG_INF).astype(F32)
        bias = band_bias + jnp.where(kj < BAND, prev_bias, 0.0)
        for pair in range(GROUP_DIM // LANES):
            cols = slice(pair * LANES, (pair + 1) * LANES)
            qp = q_ref[0, 0, pl.ds(r0, Q_BLOCK), cols]
            kp = kf_ref[pl.ds(r0, 2 * BAND), cols]
            vp = vf_ref[pl.ds(r0, 2 * BAND), cols]
            res, mx = [], []
            for hh in range(2):
                s = lax.dot_general(qp * head_mask[hh], kp, (((1,), (1,)), ((), ())),
                                    preferred_element_type=F32) + bias
                m = jnp.max(s, axis=-1, keepdims=True)
                e = jnp.exp(s - m).astype(BF16)
                res.append(_dot(e, vp * head_mask[hh] + head_mask[1 - hh]))
                mx.append(m)
            acc = jnp.where(low, res[0], res[1])
            den = pltpu.roll(jnp.where(low, res[1], res[0]), HEAD_DIM, 1)
            o_ref[0, 0, pl.ds(r0, Q_BLOCK), cols] = (acc / den).astype(BF16)
            lse_ref[0, 0, pl.ds(r0, Q_BLOCK), cols] = jnp.where(low, mx[0], mx[1]) + jnp.log(den)
        return carry

    lax.fori_loop(0, tq // Q_BLOCK, sub_block, 0)
    kf_ref[0:BAND, :] = kf_ref[tq:tq + BAND, :]
    vf_ref[0:BAND, :] = vf_ref[tq:tq + BAND, :]


def _band_attn(q, k, v):
    B, d, L, _ = q.shape
    tq = min(SEQ_TILE, L)
    spec = pl.BlockSpec((1, 1, tq, GROUP_DIM), lambda b, r, i: (b, r, i, 0))
    return pl.pallas_call(
        _band_attn_kernel,
        grid=(B, d, L // tq),
        in_specs=[spec, spec, spec],
        out_specs=[spec, spec],
        out_shape=[jax.ShapeDtypeStruct(q.shape, BF16), jax.ShapeDtypeStruct(q.shape, F32)],
        scratch_shapes=[pltpu.VMEM((BAND + tq, GROUP_DIM), BF16)] * 2,
        compiler_params=pltpu.CompilerParams(dimension_semantics=("parallel", "parallel", "arbitrary"),
                                             vmem_limit_bytes=VMEM_LIMIT),
        name=f"band_attn_d{d}",
    )(q, k, v)


def _attn_out_kernel(x_ref, p_ref, sg_ref, o0_ref, o1_ref, o2_ref, l0_ref, l1_ref, l2_ref,
                     wout_ref, png_ref, wgate_ref, wproj_ref, out_ref, perm_ref):
    ts = x_ref.shape[1]

    nblk = GROUP_DIM // LANES

    def natural(ref, g, base):
        d = DILATION_GROUPS[g][1]
        if d == 1:
            return ref[0].astype(F32)
        for j in range(nblk):
            for r in range(d):
                perm_ref[base * nblk + j, pl.ds(r, ts // d, stride=d), :] = (
                    ref[0, r, :, j * LANES:(j + 1) * LANES].astype(F32))
        return jnp.concatenate([perm_ref[base * nblk + j] for j in range(nblk)], axis=-1)

    o_refs = (o0_ref, o1_ref, o2_ref)
    l_refs = (l0_ref, l1_ref, l2_ref)
    lses = [natural(l_refs[g], g, g) for g in range(N_GROUPS)]
    m = jnp.maximum(jnp.maximum(lses[0], lses[1]), lses[2])
    wts = [jnp.exp(l - m) for l in lses]
    inv = 1.0 / (wts[0] + wts[1] + wts[2])
    o = sum(wts[g] * natural(o_refs[g], g, N_GROUPS + g) for g in range(N_GROUPS)) * inv
    y = (o * sg_ref[0].astype(F32)).astype(BF16)
    x1 = x_ref[0] + _dot(y, wout_ref[...])
    out_ref[0] = _ple_update(x1, p_ref[0], png_ref[...], wgate_ref[...], wproj_ref[...])


def _attn_out(x, p, layer, sg, os_, lses, w_out, png, w_gate, w_proj):
    B, S, D = x.shape
    ts = SEQ_TILE

    def group_spec(g):
        d = DILATION_GROUPS[g][1]
        if d == 1:
            return pl.BlockSpec((1, ts, GROUP_DIM), lambda b, s: (b, s, 0))
        return pl.BlockSpec((1, d, ts // d, GROUP_DIM), lambda b, s: (b, 0, s, 0))

    return pl.pallas_call(
        _attn_out_kernel,
        grid=(B, S // ts),
        in_specs=[pl.BlockSpec((1, ts, D), lambda b, s: (b, s, 0)),
                  pl.BlockSpec((None, 1, ts, PLE_DIM), lambda b, s: (layer, b, s, 0)),
                  pl.BlockSpec((1, ts, GROUP_DIM), lambda b, s: (b, s, 0))]
        + [group_spec(g) for g in range(N_GROUPS)] * 2
        + [_const_spec((GROUP_DIM, D)), _const_spec((1, D)), _const_spec((D, D)),
           _const_spec((PLE_DIM, D))],
        out_specs=pl.BlockSpec((1, ts, D), lambda b, s: (b, s, 0)),
        out_shape=jax.ShapeDtypeStruct((B, S, D), F32),
        scratch_shapes=[pltpu.VMEM((2 * N_GROUPS * (GROUP_DIM // LANES), ts, LANES), F32)],
        compiler_params=pltpu.CompilerParams(dimension_semantics=("parallel", "parallel"),
                                             vmem_limit_bytes=VMEM_LIMIT),
        name="attn_out",
    )(x, p, sg, *os_, *lses, w_out.astype(BF16), png.reshape(1, -1), w_gate.astype(BF16),
      w_proj.astype(BF16))


def _attn_layer(x, p, layer, ng, w_in, q_gain, k_gain, w_out, png, w_gate, w_proj, cos_t, sin_t):
    B, S, _ = x.shape
    q0, q1, q2, k0, k1, k2, v0, v1, v2, sg = _attn_in(x, ng, w_in, q_gain, k_gain, cos_t, sin_t)
    os_, lses = [], []
    for q, k, v in ((q0, k0, v0), (q1, k1, v1), (q2, k2, v2)):
        if q.ndim == 3:
            shape = (B, 1, S, GROUP_DIM)
            o, lse = _band_attn(q.reshape(shape), k.reshape(shape), v.reshape(shape))
            o, lse = o.reshape(B, S, GROUP_DIM), lse.reshape(B, S, GROUP_DIM)
        else:
            o, lse = _band_attn(q, k, v)
        os_.append(o)
        lses.append(lse)
    return _attn_out(x, p, layer, sg, os_, lses, w_out, png, w_gate, w_proj)


def kernel(x, p, positions, norm_g, conv_w_in, conv_dw, conv_dw_b, conv_ln_g, conv_ln_b, conv_w_out,
           attn_w_in, attn_q_norm, attn_k_norm, attn_w_out, ple_w_proj, ple_norm_g, ple_w_gate):
    depth = norm_g.shape[0]
    cos_t, sin_t = _rope_tables(positions)
    for i in range(depth):
        j = i // 2
        if i % 2 == 0:
            x = _conv_layer(x, p, i, norm_g[i], conv_w_in[j], conv_dw[j], conv_dw_b[j], conv_ln_g[j],
                            conv_ln_b[j], conv_w_out[j], ple_norm_g[i], ple_w_gate[i], ple_w_proj[i])
        else:
            x = _attn_layer(x, p, i, norm_g[i], attn_w_in[j], attn_q_norm[j], attn_k_norm[j],
                            attn_w_out[j], ple_norm_g[i], ple_w_gate[i], ple_w_proj[i], cos_t, sin_t)
    return x
```

```python
import functools

import jax
import jax.numpy as jnp
from jax import lax
from jax.experimental import pallas as pl
from jax.experimental.pallas import tpu as pltpu

F32 = jnp.float32
BF16 = jnp.bfloat16

D_MODEL = 1024
PLE_DIM = 256
CONV_DIM = D_MODEL
CONV_WIDTH = 31
HEAD_DIM = 64
HEADS_PER_GROUP = 8
DILATION_GROUPS = ((128, 1), (512, 4), (2048, 16))
N_GROUPS = len(DILATION_GROUPS)
GROUP_DIM = HEADS_PER_GROUP * HEAD_DIM
QKV_DIM = N_GROUPS * GROUP_DIM
BAND = 128
ROPE_THETA = 500000.0
ROT_DIM = HEAD_DIM // 4
EPS = 1e-6
NEG_INF = -1e30

LANES = 128
MXU_COLS = 256
LOG2E = 1.4426950408889634
LN2 = 0.6931471805599453
Q_SCALE = HEAD_DIM ** -0.5 * LOG2E
CONV_HALO = 32
VMEM_LIMIT = 56 * 1024 * 1024

SEQ_TILE = 512
CONV_ROWS = 128
PROJ_ROWS = 256
Q_BLOCK = 128


def _const_spec(shape):
    return pl.BlockSpec(shape, lambda *_: (0,) * len(shape), pipeline_mode=pl.Buffered(1))


def _dot(a, b):
    return jnp.dot(a, b, preferred_element_type=F32)


def _rms_norm(x, g):
    return x * lax.rsqrt(jnp.mean(x * x, axis=-1, keepdims=True) + EPS) * g


def _silu(x):
    return x * jax.nn.sigmoid(x)


def _ple_update(x1, p, png, w_gate, w_proj):
    gate = jax.nn.sigmoid(_dot(_rms_norm(x1, png).astype(BF16), w_gate))
    return x1 + _dot(p.astype(BF16), w_proj) * gate


def _rope_table_kernel(pos_ref, freq_ref, sign_ref, cos_ref, sin_ref):
    ang = pos_ref[0].astype(F32) * freq_ref[...]
    cos_ref[0] = jnp.cos(ang)
    sin_ref[0] = jnp.sin(ang) * sign_ref[...]


def _rope_tables(positions):
    B, S = positions.shape
    lane = jnp.arange(LANES) % HEAD_DIM
    half = ROT_DIM // 2
    inv_freq = 1.0 / (ROPE_THETA ** (jnp.arange(0, ROT_DIM, 2, dtype=F32) / ROT_DIM))
    freq = jnp.where(lane < ROT_DIM, inv_freq[lane % half], 0.0).astype(F32)[None, :]
    sign = jnp.where(lane < half, -1.0, jnp.where(lane < ROT_DIM, 1.0, 0.0)).astype(F32)[None, :]
    ts = SEQ_TILE
    return pl.pallas_call(
        _rope_table_kernel,
        grid=(B, S // ts),
        in_specs=[pl.BlockSpec((1, ts, 1), lambda b, s: (b, s, 0)),
                  _const_spec((1, LANES)), _const_spec((1, LANES))],
        out_specs=[pl.BlockSpec((1, ts, LANES), lambda b, s: (b, s, 0))] * 2,
        out_shape=[jax.ShapeDtypeStruct((B, S, LANES), F32)] * 2,
        compiler_params=pltpu.CompilerParams(dimension_semantics=("parallel", "parallel")),
        name="rope_tables",
    )(positions.reshape(B, S, 1), freq, sign)


def _conv_layer_kernel(x_ref, p_ref, ng_ref, win_ref, dw_ref, dwb_ref, lng_ref, lnb_ref, wout_ref,
                       png_ref, wgate_ref, wproj_ref, o_ref, ypad_ref, acc_ref, gate_ref):
    ts = x_ref.shape[1]
    c = CONV_DIM
    ncb = c // LANES
    half = CONV_ROWS // 2

    @pl.when(pl.program_id(1) == 0)
    def _():
        ypad_ref[:, 0:CONV_HALO, :] = jnp.zeros((ncb, CONV_HALO, LANES), F32)

    tap0 = CONV_HALO - (CONV_WIDTH - 1)

    for p0 in range(0, ts, PROJ_ROWS):
        prow = slice(p0, p0 + PROJ_ROWS)
        h = _rms_norm(x_ref[0, prow, :], ng_ref[...]).astype(BF16)
        y = _dot(h, win_ref[:, 0:c]) * jax.nn.sigmoid(_dot(h, win_ref[:, c:2 * c]))
        for cb in range(ncb):
            ypad_ref[cb, CONV_HALO + p0:CONV_HALO + p0 + PROJ_ROWS, :] = y[:, cb * LANES:(cb + 1) * LANES]
        gate_ref[prow, :] = _silu(_dot(h, win_ref[:, 2 * c:3 * c]))

    for rc in range(ts // CONV_ROWS):
        r0 = rc * CONV_ROWS
        rows = slice(r0, r0 + CONV_ROWS)
        for cb in range(ncb):
            cols = slice(cb * LANES, (cb + 1) * LANES)
            for phase in range(2):
                acc = jnp.broadcast_to(dwb_ref[:, cols], (half, LANES))
                for k in range(CONV_WIDTH):
                    acc = acc + dw_ref[k:k + 1, cols] * ypad_ref[cb, pl.ds(r0 + phase + tap0 + k, half, stride=2), :]
                acc_ref[cb, pl.ds(r0 + phase, half, stride=2), :] = acc
        yv = jnp.concatenate([acc_ref[cb, rows, :] for cb in range(ncb)], axis=-1)
        yc = yv - jnp.mean(yv, axis=-1, keepdims=True)
        yn = yc * lax.rsqrt(jnp.mean(yc * yc, axis=-1, keepdims=True) + EPS) * lng_ref[...] + lnb_ref[...]
        z = (_silu(yn) * gate_ref[rows, :]).astype(BF16)
        x1 = x_ref[0, rows, :] + _dot(z, wout_ref[...])
        o_ref[0, rows, :] = _ple_update(x1, p_ref[0, rows, :], png_ref[...], wgate_ref[...], wproj_ref[...])
    ypad_ref[:, 0:CONV_HALO, :] = ypad_ref[:, ts:ts + CONV_HALO, :]


def _conv_layer(x, p, layer, ng, w_in, dw, dw_b, ln_g, ln_b, w_out, png, w_gate, w_proj):
    B, S, D = x.shape
    ts = SEQ_TILE
    row = lambda v: v.reshape(1, -1).astype(F32)
    return pl.pallas_call(
        _conv_layer_kernel,
        grid=(B, S // ts),
        in_specs=[pl.BlockSpec((1, ts, D), lambda b, s: (b, s, 0)),
                  pl.BlockSpec((None, 1, ts, PLE_DIM), lambda b, s: (layer, b, s, 0)),
                  _const_spec((1, D)), _const_spec((D, 3 * CONV_DIM)),
                  _const_spec((CONV_WIDTH, CONV_DIM)), _const_spec((1, CONV_DIM)),
                  _const_spec((1, CONV_DIM)), _const_spec((1, CONV_DIM)),
                  _const_spec((CONV_DIM, D)), _const_spec((1, D)),
                  _const_spec((D, D)), _const_spec((PLE_DIM, D))],
        out_specs=pl.BlockSpec((1, ts, D), lambda b, s: (b, s, 0)),
        out_shape=jax.ShapeDtypeStruct((B, S, D), F32),
        scratch_shapes=[pltpu.VMEM((CONV_DIM // LANES, CONV_HALO + ts, LANES), F32),
                        pltpu.VMEM((CONV_DIM // LANES, ts, LANES), F32),
                        pltpu.VMEM((ts, CONV_DIM), F32)],
        compiler_params=pltpu.CompilerParams(dimension_semantics=("parallel", "arbitrary"),
                                             vmem_limit_bytes=VMEM_LIMIT),
        name="conv_layer",
    )(x, p, row(ng), w_in.astype(BF16), dw.astype(F32), row(dw_b), row(ln_g), row(ln_b),
      w_out.astype(BF16), row(png), w_gate.astype(BF16), w_proj.astype(BF16))


def _attn_in_kernel(x_ref, ng_ref, win_ref, qg_ref, kg_ref, cos_ref, sin_ref, hsum_ref,
                    q0_ref, q1_ref, q2_ref, k0_ref, k1_ref, k2_ref, v0_ref, v1_ref, v2_ref,
                    sg_ref, perm_ref):
    ts = x_ref.shape[1]
    h = _rms_norm(x_ref[0], ng_ref[...]).astype(BF16)
    cos_t = cos_ref[0]
    sin_t = sin_ref[0]
    lane = lax.broadcasted_iota(jnp.int32, (1, LANES), 1) % HEAD_DIM
    first_half = lane < (ROT_DIM // 2)
    outs = ((q0_ref, q1_ref, q2_ref), (k0_ref, k1_ref, k2_ref), (v0_ref, v1_ref, v2_ref))
    gains = (qg_ref, kg_ref)

    def emit(out_ref, g, j, val, slab):
        d = DILATION_GROUPS[g][1]
        cols = slice(j * LANES, (j + 1) * LANES)
        if d == 1:
            out_ref[0, :, cols] = val.astype(BF16)
            return
        perm_ref[slab] = val
        for r in range(d):
            out_ref[0, r, :, cols] = perm_ref[slab, pl.ds(r, ts // d, stride=d), :].astype(BF16)

    for kind in range(3):
        for g in range(N_GROUPS):
            c0 = kind * QKV_DIM + g * GROUP_DIM
            u = _dot(h, win_ref[:, c0:c0 + GROUP_DIM])
            if kind < 2:
                ss = jnp.concatenate(
                    [_dot((uw * uw).astype(BF16), hsum_ref[...])
                     for uw in (u[:, :MXU_COLS], u[:, MXU_COLS:])], axis=-1)
                u = u * lax.rsqrt(ss * (1.0 / HEAD_DIM) + EPS)
            for j in range(GROUP_DIM // LANES):
                ub = u[:, j * LANES:(j + 1) * LANES]
                if kind < 2:
                    un = ub * gains[kind][...]
                    partner = jnp.where(first_half, pltpu.roll(un, LANES - ROT_DIM // 2, 1),
                                        pltpu.roll(un, ROT_DIM // 2, 1))
                    ub = un * cos_t + partner * sin_t
                    if kind == 0:
                        ub = ub * Q_SCALE
                emit(outs[kind][g], g, j, ub, j)
    gate = _dot(h, win_ref[:, 3 * QKV_DIM:3 * QKV_DIM + GROUP_DIM])
    sg_ref[0] = _silu(gate).astype(BF16)


def _attn_in(x, ng, w_in, q_gain, k_gain, cos_t, sin_t):
    B, S, D = x.shape
    ts = SEQ_TILE
    hsum = (jnp.arange(MXU_COLS)[:, None] // HEAD_DIM == jnp.arange(MXU_COLS)[None, :] // HEAD_DIM).astype(BF16)
    lane_gain = lambda v: jnp.tile(v.astype(F32), LANES // HEAD_DIM)[None, :]

    def group_spec(g):
        d = DILATION_GROUPS[g][1]
        if d == 1:
            return pl.BlockSpec((1, ts, GROUP_DIM), lambda b, s: (b, s, 0))
        return pl.BlockSpec((1, d, ts // d, GROUP_DIM), lambda b, s: (b, 0, s, 0))

    def group_shape(g):
        d = DILATION_GROUPS[g][1]
        if d == 1:
            return jax.ShapeDtypeStruct((B, S, GROUP_DIM), BF16)
        return jax.ShapeDtypeStruct((B, d, S // d, GROUP_DIM), BF16)

    w_cols = 3 * QKV_DIM + GROUP_DIM
    return pl.pallas_call(
        _attn_in_kernel,
        grid=(B, S // ts),
        in_specs=[pl.BlockSpec((1, ts, D), lambda b, s: (b, s, 0)),
                  _const_spec((1, D)), _const_spec((D, w_cols)),
                  _const_spec((1, LANES)), _const_spec((1, LANES)),
                  pl.BlockSpec((1, ts, LANES), lambda b, s: (b, s, 0)),
                  pl.BlockSpec((1, ts, LANES), lambda b, s: (b, s, 0)),
                  _const_spec((MXU_COLS, MXU_COLS))],
        out_specs=[group_spec(g) for _ in range(3) for g in range(N_GROUPS)]
        + [pl.BlockSpec((1, ts, GROUP_DIM), lambda b, s: (b, s, 0))],
        out_shape=[group_shape(g) for _ in range(3) for g in range(N_GROUPS)]
        + [jax.ShapeDtypeStruct((B, S, GROUP_DIM), BF16)],
        scratch_shapes=[pltpu.VMEM((GROUP_DIM // LANES, ts, LANES), F32)],
        compiler_params=pltpu.CompilerParams(dimension_semantics=("parallel", "parallel"),
                                             vmem_limit_bytes=VMEM_LIMIT),
        name="attn_in",
    )(x, ng.reshape(1, -1), w_in.astype(BF16), lane_gain(q_gain), lane_gain(k_gain), cos_t, sin_t, hsum)


def _band_attn_kernel(q_ref, k_ref, v_ref, bias_ref, o_ref, lse_ref, kf_ref, vf_ref):
    tq = q_ref.shape[2]
    i = pl.program_id(2)

    @pl.when(i == 0)
    def _():
        kf_ref[0:BAND, :] = jnp.zeros((BAND, GROUP_DIM), BF16)
        vf_ref[0:BAND, :] = jnp.zeros((BAND, GROUP_DIM), BF16)

    kf_ref[BAND:BAND + tq, :] = k_ref[0, 0]
    vf_ref[BAND:BAND + tq, :] = v_ref[0, 0]

    lane = lax.broadcasted_iota(jnp.int32, (1, LANES), 1)
    low = lane < HEAD_DIM
    head_mask = (low.astype(BF16), (~low).astype(BF16))
    eye = (lax.broadcasted_iota(jnp.int32, (Q_BLOCK, Q_BLOCK), 0)
           == lax.broadcasted_iota(jnp.int32, (Q_BLOCK, Q_BLOCK), 1)).astype(BF16)
    ones = jnp.ones((2 * BAND, LANES), BF16)

    npair = GROUP_DIM // LANES
    lse_lanes = LANES // npair

    for sb in range(tq // Q_BLOCK):
        r0 = sb * Q_BLOCK
        bias_t = bias_ref[(i == 0).astype(jnp.int32)] if sb == 0 else bias_ref[0]
        lse_c = jnp.zeros((Q_BLOCK, LANES), F32)
        for pair in range(npair):
            cols = slice(pair * LANES, (pair + 1) * LANES)
            qp = q_ref[0, 0, r0:r0 + Q_BLOCK, cols]
            kp = kf_ref[r0:r0 + 2 * BAND, cols]
            vp = vf_ref[r0:r0 + 2 * BAND, cols]
            q_aug = jnp.concatenate([jnp.concatenate([qp * head_mask[hh], eye], axis=1) for hh in range(2)],
                                    axis=0)
            k_aug = jnp.concatenate([kp, bias_t], axis=1)
            s = lax.dot_general(q_aug, k_aug, (((1,), (1,)), ((), ())), preferred_element_type=F32)
            m = jnp.max(s, axis=-1, keepdims=True)
            e = jnp.exp2(s - m).astype(BF16)
            res = _dot(e, jnp.concatenate([vp, ones], axis=1))
            acc = jnp.where(low, res[:Q_BLOCK, :LANES], res[Q_BLOCK:, :LANES])
            den = jnp.where(low, res[:Q_BLOCK, LANES:], res[Q_BLOCK:, LANES:])
            mx = jnp.where(low, m[:Q_BLOCK], m[Q_BLOCK:])
            o_ref[0, 0, r0:r0 + Q_BLOCK, cols] = (acc / den).astype(BF16)
            lse = (mx + jnp.log2(den)) * LN2
            shift = (lse_lanes * pair + lse_lanes // 2 - HEAD_DIM) % LANES
            lse_c = jnp.where(lane // lse_lanes == pair, pltpu.roll(lse, shift, 1), lse_c)
        lse_ref[0, 0, r0:r0 + Q_BLOCK, :] = lse_c

    kf_ref[0:BAND, :] = kf_ref[tq:tq + BAND, :]
    vf_ref[0:BAND, :] = vf_ref[tq:tq + BAND, :]


def _band_bias():
    kj = jnp.arange(2 * BAND)[:, None]
    qi = jnp.arange(Q_BLOCK)[None, :]
    dist = BAND + qi - kj
    band = (dist >= 0) & (dist <= BAND)
    both = jnp.stack([band, band & (kj >= BAND)])
    return jnp.where(both, 0.0, NEG_INF).astype(BF16)


def _band_attn(q, k, v):
    B, d, L, _ = q.shape
    tq = min(SEQ_TILE, L)
    spec = pl.BlockSpec((1, 1, tq, GROUP_DIM), lambda b, r, i: (b, r, i, 0))
    lse_spec = pl.BlockSpec((1, 1, tq, LANES), lambda b, r, i: (b, r, i, 0))
    return pl.pallas_call(
        _band_attn_kernel,
        grid=(B, d, L // tq),
        in_specs=[spec, spec, spec, _const_spec((2, 2 * BAND, Q_BLOCK))],
        out_specs=[spec, lse_spec],
        out_shape=[jax.ShapeDtypeStruct(q.shape, BF16), jax.ShapeDtypeStruct((B, d, L, LANES), F32)],
        scratch_shapes=[pltpu.VMEM((BAND + tq, GROUP_DIM), BF16)] * 2,
        compiler_params=pltpu.CompilerParams(dimension_semantics=("parallel", "parallel", "arbitrary"),
                                             vmem_limit_bytes=VMEM_LIMIT),
        name=f"band_attn_d{d}",
    )(q, k, v, _band_bias())


def _attn_out_kernel(x_ref, p_ref, sg_ref, o0_ref, o1_ref, o2_ref, l0_ref, l1_ref, l2_ref,
                     expand_ref, wout_ref, png_ref, wgate_ref, wproj_ref, out_ref, perm_ref):
    ts = x_ref.shape[1]

    def natural(ref, g, slab0):
        d = DILATION_GROUPS[g][1]
        nblk = ref.shape[-1] // LANES
        if d == 1:
            return ref[0].astype(F32)
        for j in range(nblk):
            for r in range(d):
                perm_ref[slab0 + j, pl.ds(r, ts // d, stride=d), :] = (
                    ref[0, r, :, j * LANES:(j + 1) * LANES].astype(F32))
        return jnp.concatenate([perm_ref[slab0 + j] for j in range(nblk)], axis=-1)

    o_refs = (o0_ref, o1_ref, o2_ref)
    l_refs = (l0_ref, l1_ref, l2_ref)
    o_slabs = GROUP_DIM // LANES
    lses = [natural(l_refs[g], g, N_GROUPS * o_slabs + g) for g in range(N_GROUPS)]
    m = jnp.maximum(jnp.maximum(lses[0], lses[1]), lses[2])
    wts = [jnp.exp(l - m) for l in lses]
    inv = 1.0 / (wts[0] + wts[1] + wts[2])
    o = sum(_dot((wts[g] * inv).astype(BF16), expand_ref[...]) * natural(o_refs[g], g, g * o_slabs)
            for g in range(N_GROUPS))
    y = (o * sg_ref[0].astype(F32)).astype(BF16)
    x1 = x_ref[0] + _dot(y, wout_ref[...])
    out_ref[0] = _ple_update(x1, p_ref[0], png_ref[...], wgate_ref[...], wproj_ref[...])


def _attn_out(x, p, layer, sg, os_, lses, w_out, png, w_gate, w_proj):
    B, S, D = x.shape
    ts = SEQ_TILE

    def group_spec(g, width):
        d = DILATION_GROUPS[g][1]
        if d == 1:
            return pl.BlockSpec((1, ts, width), lambda b, s: (b, s, 0))
        return pl.BlockSpec((1, d, ts // d, width), lambda b, s: (b, 0, s, 0))

    lanes_per_head = LANES // HEADS_PER_GROUP
    expand = (jnp.arange(LANES)[:, None]
              == (jnp.arange(GROUP_DIM)[None, :] // HEAD_DIM) * lanes_per_head + lanes_per_head // 2).astype(BF16)
    return pl.pallas_call(
        _attn_out_kernel,
        grid=(B, S // ts),
        in_specs=[pl.BlockSpec((1, ts, D), lambda b, s: (b, s, 0)),
                  pl.BlockSpec((None, 1, ts, PLE_DIM), lambda b, s: (layer, b, s, 0)),
                  pl.BlockSpec((1, ts, GROUP_DIM), lambda b, s: (b, s, 0))]
        + [group_spec(g, GROUP_DIM) for g in range(N_GROUPS)]
        + [group_spec(g, LANES) for g in range(N_GROUPS)]
        + [_const_spec((LANES, GROUP_DIM)), _const_spec((GROUP_DIM, D)), _const_spec((1, D)),
           _const_spec((D, D)), _const_spec((PLE_DIM, D))],
        out_specs=pl.BlockSpec((1, ts, D), lambda b, s: (b, s, 0)),
        out_shape=jax.ShapeDtypeStruct((B, S, D), F32),
        scratch_shapes=[pltpu.VMEM((N_GROUPS * (GROUP_DIM // LANES + 1), ts, LANES), F32)],
        compiler_params=pltpu.CompilerParams(dimension_semantics=("parallel", "parallel"),
                                             vmem_limit_bytes=VMEM_LIMIT),
        name="attn_out",
    )(x, p, sg, *os_, *lses, expand, w_out.astype(BF16), png.reshape(1, -1), w_gate.astype(BF16),
      w_proj.astype(BF16))


def _attn_layer(x, p, layer, ng, w_in, q_gain, k_gain, w_out, png, w_gate, w_proj, cos_t, sin_t):
    B, S, _ = x.shape
    q0, q1, q2, k0, k1, k2, v0, v1, v2, sg = _attn_in(x, ng, w_in, q_gain, k_gain, cos_t, sin_t)
    os_, lses = [], []
    for q, k, v in ((q0, k0, v0), (q1, k1, v1), (q2, k2, v2)):
        if q.ndim == 3:
            shape = (B, 1, S, GROUP_DIM)
            o, lse = _band_attn(q.reshape(shape), k.reshape(shape), v.reshape(shape))
            o, lse = o.reshape(B, S, GROUP_DIM), lse.reshape(B, S, LANES)
        else:
            o, lse = _band_attn(q, k, v)
        os_.append(o)
        lses.append(lse)
    return _attn_out(x, p, layer, sg, os_, lses, w_out, png, w_gate, w_proj)


def kernel(x, p, positions, norm_g, conv_w_in, conv_dw, conv_dw_b, conv_ln_g, conv_ln_b, conv_w_out,
           attn_w_in, attn_q_norm, attn_k_norm, attn_w_out, ple_w_proj, ple_norm_g, ple_w_gate):
    depth = norm_g.shape[0]
    cos_t, sin_t = _rope_tables(positions)
    for i in range(depth):
        j = i // 2
        if i % 2 == 0:
            x = _conv_layer(x, p, i, norm_g[i], conv_w_in[j], conv_dw[j], conv_dw_b[j], conv_ln_g[j],
                            conv_ln_b[j], conv_w_out[j], ple_norm_g[i], ple_w_gate[i], ple_w_proj[i])
        else:
            x = _attn_layer(x, p, i, norm_g[i], attn_w_in[j], attn_q_norm[j], attn_k_norm[j],
                            attn_w_out[j], ple_norm_g[i], ple_w_gate[i], ple_w_proj[i], cos_t, sin_t)
    return x
```

```python
import functools

import jax
import jax.numpy as jnp
from jax import lax
from jax.experimental import pallas as pl
from jax.experimental.pallas import tpu as pltpu

F32 = jnp.float32
BF16 = jnp.bfloat16

D_MODEL = 1024
PLE_DIM = 256
CONV_DIM = D_MODEL
CONV_WIDTH = 31
HEAD_DIM = 64
HEADS_PER_GROUP = 8
DILATION_GROUPS = ((128, 1), (512, 4), (2048, 16))
N_GROUPS = len(DILATION_GROUPS)
GROUP_DIM = HEADS_PER_GROUP * HEAD_DIM
QKV_DIM = N_GROUPS * GROUP_DIM
BAND = 128
ROPE_THETA = 500000.0
ROT_DIM = HEAD_DIM // 4
EPS = 1e-6
NEG_INF = -1e30

LANES = 128
MXU_COLS = 256
LOG2E = 1.4426950408889634
LN2 = 0.6931471805599453
Q_SCALE = HEAD_DIM ** -0.5 * LOG2E
CONV_HALO = 32
VMEM_LIMIT = 56 * 1024 * 1024

SEQ_TILE = 512
CONV_ROWS = 128
PROJ_ROWS = 256
Q_BLOCK = 128


def _const_spec(shape):
    return pl.BlockSpec(shape, lambda *_: (0,) * len(shape), pipeline_mode=pl.Buffered(1))


def _dot(a, b):
    return jnp.dot(a, b, preferred_element_type=F32)


def _rms_norm(x, g):
    return x * lax.rsqrt(jnp.mean(x * x, axis=-1, keepdims=True) + EPS) * g


def _silu(x):
    return x * jax.nn.sigmoid(x)


def _ple_update(x1, p, png, w_gate, w_proj):
    gate = jax.nn.sigmoid(_dot(_rms_norm(x1, png).astype(BF16), w_gate))
    return x1 + _dot(p.astype(BF16), w_proj) * gate


def _rope_table_kernel(pos_ref, freq_ref, sign_ref, cos_ref, sin_ref):
    ang = pos_ref[0].astype(F32) * freq_ref[...]
    cos_ref[0] = jnp.cos(ang)
    sin_ref[0] = jnp.sin(ang) * sign_ref[...]


def _rope_tables(positions):
    B, S = positions.shape
    lane = jnp.arange(LANES) % HEAD_DIM
    half = ROT_DIM // 2
    inv_freq = 1.0 / (ROPE_THETA ** (jnp.arange(0, ROT_DIM, 2, dtype=F32) / ROT_DIM))
    freq = jnp.where(lane < ROT_DIM, inv_freq[lane % half], 0.0).astype(F32)[None, :]
    sign = jnp.where(lane < half, -1.0, jnp.where(lane < ROT_DIM, 1.0, 0.0)).astype(F32)[None, :]
    ts = SEQ_TILE
    return pl.pallas_call(
        _rope_table_kernel,
        grid=(B, S // ts),
        in_specs=[pl.BlockSpec((1, ts, 1), lambda b, s: (b, s, 0)),
                  _const_spec((1, LANES)), _const_spec((1, LANES))],
        out_specs=[pl.BlockSpec((1, ts, LANES), lambda b, s: (b, s, 0))] * 2,
        out_shape=[jax.ShapeDtypeStruct((B, S, LANES), F32)] * 2,
        compiler_params=pltpu.CompilerParams(dimension_semantics=("parallel", "parallel")),
        name="rope_tables",
    )(positions.reshape(B, S, 1), freq, sign)


def _conv_layer_kernel(x_ref, p_ref, ng_ref, win_ref, dw_ref, dwb_ref, lng_ref, lnb_ref, wout_ref,
                       png_ref, wgate_ref, wproj_ref, o_ref, ypad_ref, acc_ref, gate_ref):
    ts = x_ref.shape[1]
    c = CONV_DIM
    ncb = c // LANES
    half = CONV_ROWS // 2

    @pl.when(pl.program_id(1) == 0)
    def _():
        ypad_ref[:, 0:CONV_HALO, :] = jnp.zeros((ncb, CONV_HALO, LANES), F32)

    tap0 = CONV_HALO - (CONV_WIDTH - 1)

    for p0 in range(0, ts, PROJ_ROWS):
        prow = slice(p0, p0 + PROJ_ROWS)
        h = _rms_norm(x_ref[0, prow, :], ng_ref[...]).astype(BF16)
        y = _dot(h, win_ref[:, 0:c]) * jax.nn.sigmoid(_dot(h, win_ref[:, c:2 * c]))
        for cb in range(ncb):
            ypad_ref[cb, CONV_HALO + p0:CONV_HALO + p0 + PROJ_ROWS, :] = y[:, cb * LANES:(cb + 1) * LANES]
        gate_ref[prow, :] = _silu(_dot(h, win_ref[:, 2 * c:3 * c]))

    for rc in range(ts // CONV_ROWS):
        r0 = rc * CONV_ROWS
        rows = slice(r0, r0 + CONV_ROWS)
        for cb in range(ncb):
            cols = slice(cb * LANES, (cb + 1) * LANES)
            for phase in range(2):
                acc = jnp.broadcast_to(dwb_ref[:, cols], (half, LANES))
                for k in range(CONV_WIDTH):
                    acc = acc + dw_ref[k:k + 1, cols] * ypad_ref[cb, pl.ds(r0 + phase + tap0 + k, half, stride=2), :]
                acc_ref[cb, pl.ds(r0 + phase, half, stride=2), :] = acc
        yv = jnp.concatenate([acc_ref[cb, rows, :] for cb in range(ncb)], axis=-1)
        yc = yv - jnp.mean(yv, axis=-1, keepdims=True)
        yn = yc * lax.rsqrt(jnp.mean(yc * yc, axis=-1, keepdims=True) + EPS) * lng_ref[...] + lnb_ref[...]
        z = (_silu(yn) * gate_ref[rows, :]).astype(BF16)
        x1 = x_ref[0, rows, :] + _dot(z, wout_ref[...])
        o_ref[0, rows, :] = _ple_update(x1, p_ref[0, rows, :], png_ref[...], wgate_ref[...], wproj_ref[...])
    ypad_ref[:, 0:CONV_HALO, :] = ypad_ref[:, ts:ts + CONV_HALO, :]


def _conv_layer(x, p, layer, ng, w_in, dw, dw_b, ln_g, ln_b, w_out, png, w_gate, w_proj):
    B, S, D = x.shape
    ts = SEQ_TILE
    row = lambda v: v.reshape(1, -1).astype(F32)
    return pl.pallas_call(
        _conv_layer_kernel,
        grid=(B, S // ts),
        in_specs=[pl.BlockSpec((1, ts, D), lambda b, s: (b, s, 0)),
                  pl.BlockSpec((None, 1, ts, PLE_DIM), lambda b, s: (layer, b, s, 0)),
                  _const_spec((1, D)), _const_spec((D, 3 * CONV_DIM)),
                  _const_spec((CONV_WIDTH, CONV_DIM)), _const_spec((1, CONV_DIM)),
                  _const_spec((1, CONV_DIM)), _const_spec((1, CONV_DIM)),
                  _const_spec((CONV_DIM, D)), _const_spec((1, D)),
                  _const_spec((D, D)), _const_spec((PLE_DIM, D))],
        out_specs=pl.BlockSpec((1, ts, D), lambda b, s: (b, s, 0)),
        out_shape=jax.ShapeDtypeStruct((B, S, D), F32),
        scratch_shapes=[pltpu.VMEM((CONV_DIM // LANES, CONV_HALO + ts, LANES), F32),
                        pltpu.VMEM((CONV_DIM // LANES, ts, LANES), F32),
                        pltpu.VMEM((ts, CONV_DIM), F32)],
        compiler_params=pltpu.CompilerParams(dimension_semantics=("parallel", "arbitrary"),
                                             vmem_limit_bytes=VMEM_LIMIT),
        name="conv_layer",
    )(x, p, row(ng), w_in.astype(BF16), dw.astype(F32), row(dw_b), row(ln_g), row(ln_b),
      w_out.astype(BF16), row(png), w_gate.astype(BF16), w_proj.astype(BF16))


def _attn_in_kernel(x_ref, ng_ref, win_ref, qg_ref, kg_ref, cos_ref, sin_ref, hsum_ref,
                    q0_ref, q1_ref, q2_ref, k0_ref, k1_ref, k2_ref, v0_ref, v1_ref, v2_ref,
                    sg_ref, perm_ref):
    ts = x_ref.shape[1]
    h = _rms_norm(x_ref[0], ng_ref[...]).astype(BF16)
    cos_t = cos_ref[0]
    sin_t = sin_ref[0]
    lane = lax.broadcasted_iota(jnp.int32, (1, LANES), 1) % HEAD_DIM
    first_half = lane < (ROT_DIM // 2)
    outs = ((q0_ref, q1_ref, q2_ref), (k0_ref, k1_ref, k2_ref), (v0_ref, v1_ref, v2_ref))
    gains = (qg_ref, kg_ref)

    def emit(out_ref, g, j, val, slab):
        d = DILATION_GROUPS[g][1]
        cols = slice(j * LANES, (j + 1) * LANES)
        if d == 1:
            out_ref[0, :, cols] = val.astype(BF16)
            return
        perm_ref[slab] = val
        for r in range(d):
            out_ref[0, r, :, cols] = perm_ref[slab, pl.ds(r, ts // d, stride=d), :].astype(BF16)

    for kind in range(3):
        for g in range(N_GROUPS):
            c0 = kind * QKV_DIM + g * GROUP_DIM
            u = _dot(h, win_ref[:, c0:c0 + GROUP_DIM])
            if kind < 2:
                ss = jnp.concatenate(
                    [_dot((uw * uw).astype(BF16), hsum_ref[...])
                     for uw in (u[:, :MXU_COLS], u[:, MXU_COLS:])], axis=-1)
                u = u * lax.rsqrt(ss * (1.0 / HEAD_DIM) + EPS)
            for j in range(GROUP_DIM // LANES):
                ub = u[:, j * LANES:(j + 1) * LANES]
                if kind < 2:
                    un = ub * gains[kind][...]
                    partner = jnp.where(first_half, pltpu.roll(un, LANES - ROT_DIM // 2, 1),
                                        pltpu.roll(un, ROT_DIM // 2, 1))
                    ub = un * cos_t + partner * sin_t
                    if kind == 0:
                        ub = ub * Q_SCALE
                emit(outs[kind][g], g, j, ub, j)
    gate = _dot(h, win_ref[:, 3 * QKV_DIM:3 * QKV_DIM + GROUP_DIM])
    sg_ref[0] = _silu(gate).astype(BF16)


def _attn_in(x, ng, w_in, q_gain, k_gain, cos_t, sin_t):
    B, S, D = x.shape
    ts = SEQ_TILE
    hsum = (jnp.arange(MXU_COLS)[:, None] // HEAD_DIM == jnp.arange(MXU_COLS)[None, :] // HEAD_DIM).astype(BF16)
    lane_gain = lambda v: jnp.tile(v.astype(F32), LANES // HEAD_DIM)[None, :]

    def group_spec(g):
        d = DILATION_GROUPS[g][1]
        if d == 1:
            return pl.BlockSpec((1, ts, GROUP_DIM), lambda b, s: (b, s, 0))
        return pl.BlockSpec((1, d, ts // d, GROUP_DIM), lambda b, s: (b, 0, s, 0))

    def group_shape(g):
        d = DILATION_GROUPS[g][1]
        if d == 1:
            return jax.ShapeDtypeStruct((B, S, GROUP_DIM), BF16)
        return jax.ShapeDtypeStruct((B, d, S // d, GROUP_DIM), BF16)

    w_cols = 3 * QKV_DIM + GROUP_DIM
    return pl.pallas_call(
        _attn_in_kernel,
        grid=(B, S // ts),
        in_specs=[pl.BlockSpec((1, ts, D), lambda b, s: (b, s, 0)),
                  _const_spec((1, D)), _const_spec((D, w_cols)),
                  _const_spec((1, LANES)), _const_spec((1, LANES)),
                  pl.BlockSpec((1, ts, LANES), lambda b, s: (b, s, 0)),
                  pl.BlockSpec((1, ts, LANES), lambda b, s: (b, s, 0)),
                  _const_spec((MXU_COLS, MXU_COLS))],
        out_specs=[group_spec(g) for _ in range(3) for g in range(N_GROUPS)]
        + [pl.BlockSpec((1, ts, GROUP_DIM), lambda b, s: (b, s, 0))],
        out_shape=[group_shape(g) for _ in range(3) for g in range(N_GROUPS)]
        + [jax.ShapeDtypeStruct((B, S, GROUP_DIM), BF16)],
        scratch_shapes=[pltpu.VMEM((GROUP_DIM // LANES, ts, LANES), F32)],
        compiler_params=pltpu.CompilerParams(dimension_semantics=("parallel", "parallel"),
                                             vmem_limit_bytes=VMEM_LIMIT),
        name="attn_in",
    )(x, ng.reshape(1, -1), w_in.astype(BF16), lane_gain(q_gain), lane_gain(k_gain), cos_t, sin_t, hsum)


def _band_attn_kernel(q_ref, k_ref, v_ref, bias_ref, o_ref, lse_ref, kf_ref, vf_ref):
    nres, tq = q_ref.shape[1], q_ref.shape[2]
    i = pl.program_id(2)

    @pl.when(i == 0)
    def _():
        kf_ref[:, 0:BAND, :] = jnp.zeros((nres, BAND, GROUP_DIM), BF16)
        vf_ref[:, 0:BAND, :] = jnp.zeros((nres, BAND, GROUP_DIM), BF16)

    kf_ref[:, BAND:BAND + tq, :] = k_ref[0]
    vf_ref[:, BAND:BAND + tq, :] = v_ref[0]

    lane = lax.broadcasted_iota(jnp.int32, (1, LANES), 1)
    low = lane < HEAD_DIM
    head_mask = (low.astype(BF16), (~low).astype(BF16))
    eye = (lax.broadcasted_iota(jnp.int32, (Q_BLOCK, Q_BLOCK), 0)
           == lax.broadcasted_iota(jnp.int32, (Q_BLOCK, Q_BLOCK), 1)).astype(BF16)
    ones = jnp.ones((2 * BAND, LANES), BF16)

    npair = GROUP_DIM // LANES
    lse_lanes = LANES // npair

    units = [(res, sb * Q_BLOCK, pair) for res in range(nres) for sb in range(tq // Q_BLOCK)
             for pair in range(npair)]

    scores = []
    for res, r0, pair in units:
        cols = slice(pair * LANES, (pair + 1) * LANES)
        qp = q_ref[0, res, r0:r0 + Q_BLOCK, cols]
        kp = kf_ref[res, r0:r0 + 2 * BAND, cols]
        bias_t = bias_ref[(i == 0).astype(jnp.int32)] if r0 == 0 else bias_ref[0]
        q_aug = jnp.concatenate([jnp.concatenate([qp * head_mask[hh], eye], axis=1) for hh in range(2)],
                                axis=0)
        k_aug = jnp.concatenate([kp, bias_t], axis=1)
        scores.append(lax.dot_general(q_aug, k_aug, (((1,), (1,)), ((), ())), preferred_element_type=F32))

    probs, maxes = [], []
    for s in scores:
        m = jnp.max(s, axis=-1, keepdims=True)
        probs.append(jnp.exp2(s - m).astype(BF16))
        maxes.append(m)

    lse_c = None
    for (res, r0, pair), e, m in zip(units, probs, maxes):
        cols = slice(pair * LANES, (pair + 1) * LANES)
        vp = vf_ref[res, r0:r0 + 2 * BAND, cols]
        out = _dot(e, jnp.concatenate([vp, ones], axis=1))
        acc = jnp.where(low, out[:Q_BLOCK, :LANES], out[Q_BLOCK:, :LANES])
        den = jnp.where(low, out[:Q_BLOCK, LANES:], out[Q_BLOCK:, LANES:])
        mx = jnp.where(low, m[:Q_BLOCK], m[Q_BLOCK:])
        o_ref[0, res, r0:r0 + Q_BLOCK, cols] = (acc / den).astype(BF16)
        lse = (mx + jnp.log2(den)) * LN2
        shift = (lse_lanes * pair + lse_lanes // 2 - HEAD_DIM) % LANES
        rolled = pltpu.roll(lse, shift, 1)
        lse_c = rolled if pair == 0 else jnp.where(lane // lse_lanes == pair, rolled, lse_c)
        if pair == npair - 1:
            lse_ref[0, res, r0:r0 + Q_BLOCK, :] = lse_c

    kf_ref[:, 0:BAND, :] = kf_ref[:, tq:tq + BAND, :]
    vf_ref[:, 0:BAND, :] = vf_ref[:, tq:tq + BAND, :]


def _band_bias():
    kj = jnp.arange(2 * BAND)[:, None]
    qi = jnp.arange(Q_BLOCK)[None, :]
    dist = BAND + qi - kj
    band = (dist >= 0) & (dist <= BAND)
    both = jnp.stack([band, band & (kj >= BAND)])
    return jnp.where(both, 0.0, NEG_INF).astype(BF16)


def _band_attn(q, k, v):
    B, d, L, _ = q.shape
    tq = min(SEQ_TILE, L)
    nres = min(d, SEQ_TILE // tq)
    spec = pl.BlockSpec((1, nres, tq, GROUP_DIM), lambda b, r, i: (b, r, i, 0))
    lse_spec = pl.BlockSpec((1, nres, tq, LANES), lambda b, r, i: (b, r, i, 0))
    return pl.pallas_call(
        _band_attn_kernel,
        grid=(B, d // nres, L // tq),
        in_specs=[spec, spec, spec, _const_spec((2, 2 * BAND, Q_BLOCK))],
        out_specs=[spec, lse_spec],
        out_shape=[jax.ShapeDtypeStruct(q.shape, BF16), jax.ShapeDtypeStruct((B, d, L, LANES), F32)],
        scratch_shapes=[pltpu.VMEM((nres, BAND + tq, GROUP_DIM), BF16)] * 2,
        compiler_params=pltpu.CompilerParams(dimension_semantics=("parallel", "parallel", "arbitrary"),
                                             vmem_limit_bytes=VMEM_LIMIT),
        name=f"band_attn_d{d}",
    )(q, k, v, _band_bias())


def _attn_out_kernel(x_ref, p_ref, sg_ref, o0_ref, o1_ref, o2_ref, l0_ref, l1_ref, l2_ref,
                     expand_ref, wout_ref, png_ref, wgate_ref, wproj_ref, out_ref, perm_ref):
    ts = x_ref.shape[1]

    def natural(ref, g, slab0):
        d = DILATION_GROUPS[g][1]
        nblk = ref.shape[-1] // LANES
        if d == 1:
            return ref[0].astype(F32)
        for j in range(nblk):
            for r in range(d):
                perm_ref[slab0 + j, pl.ds(r, ts // d, stride=d), :] = (
                    ref[0, r, :, j * LANES:(j + 1) * LANES].astype(F32))
        return jnp.concatenate([perm_ref[slab0 + j] for j in range(nblk)], axis=-1)

    o_refs = (o0_ref, o1_ref, o2_ref)
    l_refs = (l0_ref, l1_ref, l2_ref)
    o_slabs = GROUP_DIM // LANES
    lses = [natural(l_refs[g], g, N_GROUPS * o_slabs + g) for g in range(N_GROUPS)]
    m = jnp.maximum(jnp.maximum(lses[0], lses[1]), lses[2])
    wts = [jnp.exp(l - m) for l in lses]
    inv = 1.0 / (wts[0] + wts[1] + wts[2])
    o = sum(_dot((wts[g] * inv).astype(BF16), expand_ref[...]) * natural(o_refs[g], g, g * o_slabs)
            for g in range(N_GROUPS))
    y = (o * sg_ref[0].astype(F32)).astype(BF16)
    x1 = x_ref[0] + _dot(y, wout_ref[...])
    out_ref[0] = _ple_update(x1, p_ref[0], png_ref[...], wgate_ref[...], wproj_ref[...])


def _attn_out(x, p, layer, sg, os_, lses, w_out, png, w_gate, w_proj):
    B, S, D = x.shape
    ts = SEQ_TILE

    def group_spec(g, width):
        d = DILATION_GROUPS[g][1]
        if d == 1:
            return pl.BlockSpec((1, ts, width), lambda b, s: (b, s, 0))
        return pl.BlockSpec((1, d, ts // d, width), lambda b, s: (b, 0, s, 0))

    lanes_per_head = LANES // HEADS_PER_GROUP
    expand = (jnp.arange(LANES)[:, None]
              == (jnp.arange(GROUP_DIM)[None, :] // HEAD_DIM) * lanes_per_head + lanes_per_head // 2).astype(BF16)
    return pl.pallas_call(
        _attn_out_kernel,
        grid=(B, S // ts),
        in_specs=[pl.BlockSpec((1, ts, D), lambda b, s: (b, s, 0)),
                  pl.BlockSpec((None, 1, ts, PLE_DIM), lambda b, s: (layer, b, s, 0)),
                  pl.BlockSpec((1, ts, GROUP_DIM), lambda b, s: (b, s, 0))]
        + [group_spec(g, GROUP_DIM) for g in range(N_GROUPS)]
        + [group_spec(g, LANES) for g in range(N_GROUPS)]
        + [_const_spec((LANES, GROUP_DIM)), _const_spec((GROUP_DIM, D)), _const_spec((1, D)),
           _const_spec((D, D)), _const_spec((PLE_DIM, D))],
        out_specs=pl.BlockSpec((1, ts, D), lambda b, s: (b, s, 0)),
        out_shape=jax.ShapeDtypeStruct((B, S, D), F32),
        scratch_shapes=[pltpu.VMEM((N_GROUPS * (GROUP_DIM // LANES + 1), ts, LANES), F32)],
        compiler_params=pltpu.CompilerParams(dimension_semantics=("parallel", "parallel"),
                                             vmem_limit_bytes=VMEM_LIMIT),
        name="attn_out",
    )(x, p, sg, *os_, *lses, expand, w_out.astype(BF16), png.reshape(1, -1), w_gate.astype(BF16),
      w_proj.astype(BF16))


def _attn_layer(x, p, layer, ng, w_in, q_gain, k_gain, w_out, png, w_gate, w_proj, cos_t, sin_t):
    B, S, _ = x.shape
    q0, q1, q2, k0, k1, k2, v0, v1, v2, sg = _attn_in(x, ng, w_in, q_gain, k_gain, cos_t, sin_t)
    os_, lses = [], []
    for q, k, v in ((q0, k0, v0), (q1, k1, v1), (q2, k2, v2)):
        if q.ndim == 3:
            shape = (B, 1, S, GROUP_DIM)
            o, lse = _band_attn(q.reshape(shape), k.reshape(shape), v.reshape(shape))
            o, lse = o.reshape(B, S, GROUP_DIM), lse.reshape(B, S, LANES)
        else:
            o, lse = _band_attn(q, k, v)
        os_.append(o)
        lses.append(lse)
    return _attn_out(x, p, layer, sg, os_, lses, w_out, png, w_gate, w_proj)


def kernel(x, p, positions, norm_g, conv_w_in, conv_dw, conv_dw_b, conv_ln_g, conv_ln_b, conv_w_out,
           attn_w_in, attn_q_norm, attn_k_norm, attn_w_out, ple_w_proj, ple_norm_g, ple_w_gate):
    depth = norm_g.shape[0]
    cos_t, sin_t = _rope_tables(positions)
    for i in range(depth):
        j = i // 2
        if i % 2 == 0:
            x = _conv_layer(x, p, i, norm_g[i], conv_w_in[j], conv_dw[j], conv_dw_b[j], conv_ln_g[j],
                            conv_ln_b[j], conv_w_out[j], ple_norm_g[i], ple_w_gate[i], ple_w_proj[i])
        else:
            x = _attn_layer(x, p, i, norm_g[i], attn_w_in[j], attn_q_norm[j], attn_k_norm[j],
                            attn_w_out[j], ple_norm_g[i], ple_w_gate[i], ple_w_proj[i], cos_t, sin_t)
    return x
```

```python
import functools

import jax
import jax.numpy as jnp
from jax import lax
from jax.experimental import pallas as pl
from jax.experimental.pallas import tpu as pltpu

F32 = jnp.float32
BF16 = jnp.bfloat16

D_MODEL = 1024
PLE_DIM = 256
CONV_DIM = D_MODEL
CONV_WIDTH = 31
HEAD_DIM = 64
HEADS_PER_GROUP = 8
DILATION_GROUPS = ((128, 1), (512, 4), (2048, 16))
N_GROUPS = len(DILATION_GROUPS)
GROUP_DIM = HEADS_PER_GROUP * HEAD_DIM
QKV_DIM = N_GROUPS * GROUP_DIM
BAND = 128
ROPE_THETA = 500000.0
ROT_DIM = HEAD_DIM // 4
EPS = 1e-6
NEG_INF = -1e30

LANES = 128
MXU_COLS = 256
LOG2E = 1.4426950408889634
LN2 = 0.6931471805599453
Q_SCALE = HEAD_DIM ** -0.5 * LOG2E
CONV_HALO = 32
VMEM_LIMIT = 56 * 1024 * 1024

SEQ_TILE = 512
CONV_ROWS = 128
PROJ_ROWS = 256
Q_BLOCK = 128
BAND_ROWS = 2048


def _const_spec(shape):
    return pl.BlockSpec(shape, lambda *_: (0,) * len(shape), pipeline_mode=pl.Buffered(1))


def _dot(a, b):
    return jnp.dot(a, b, preferred_element_type=F32)


def _rms_norm(x, g):
    return x * lax.rsqrt(jnp.mean(x * x, axis=-1, keepdims=True) + EPS) * g


def _silu(x):
    return x * jax.nn.sigmoid(x)


def _ple_update(x1, p, png, w_gate, w_proj):
    gate = jax.nn.sigmoid(_dot(_rms_norm(x1, png).astype(BF16), w_gate))
    return x1 + _dot(p.astype(BF16), w_proj) * gate


def _rope_table_kernel(pos_ref, freq_ref, sign_ref, cos_ref, sin_ref):
    ang = pos_ref[0].astype(F32) * freq_ref[...]
    cos_ref[0] = jnp.cos(ang)
    sin_ref[0] = jnp.sin(ang) * sign_ref[...]


def _rope_tables(positions):
    B, S = positions.shape
    lane = jnp.arange(LANES) % HEAD_DIM
    half = ROT_DIM // 2
    inv_freq = 1.0 / (ROPE_THETA ** (jnp.arange(0, ROT_DIM, 2, dtype=F32) / ROT_DIM))
    freq = jnp.where(lane < ROT_DIM, inv_freq[lane % half], 0.0).astype(F32)[None, :]
    sign = jnp.where(lane < half, -1.0, jnp.where(lane < ROT_DIM, 1.0, 0.0)).astype(F32)[None, :]
    ts = SEQ_TILE
    return pl.pallas_call(
        _rope_table_kernel,
        grid=(B, S // ts),
        in_specs=[pl.BlockSpec((1, ts, 1), lambda b, s: (b, s, 0)),
                  _const_spec((1, LANES)), _const_spec((1, LANES))],
        out_specs=[pl.BlockSpec((1, ts, LANES), lambda b, s: (b, s, 0))] * 2,
        out_shape=[jax.ShapeDtypeStruct((B, S, LANES), F32)] * 2,
        compiler_params=pltpu.CompilerParams(dimension_semantics=("parallel", "parallel")),
        name="rope_tables",
    )(positions.reshape(B, S, 1), freq, sign)


def _conv_layer_kernel(x_ref, p_ref, ng_ref, win_ref, dw_ref, dwb_ref, lng_ref, lnb_ref, wout_ref,
                       png_ref, wgate_ref, wproj_ref, o_ref, ypad_ref, acc_ref, gate_ref):
    ts = x_ref.shape[1]
    c = CONV_DIM
    ncb = c // LANES
    half = CONV_ROWS // 2

    @pl.when(pl.program_id(1) == 0)
    def _():
        ypad_ref[:, 0:CONV_HALO, :] = jnp.zeros((ncb, CONV_HALO, LANES), F32)

    tap0 = CONV_HALO - (CONV_WIDTH - 1)

    for p0 in range(0, ts, PROJ_ROWS):
        prow = slice(p0, p0 + PROJ_ROWS)
        h = _rms_norm(x_ref[0, prow, :], ng_ref[...]).astype(BF16)
        y = _dot(h, win_ref[:, 0:c]) * jax.nn.sigmoid(_dot(h, win_ref[:, c:2 * c]))
        for cb in range(ncb):
            ypad_ref[cb, CONV_HALO + p0:CONV_HALO + p0 + PROJ_ROWS, :] = y[:, cb * LANES:(cb + 1) * LANES]
        gate_ref[prow, :] = _silu(_dot(h, win_ref[:, 2 * c:3 * c]))

    for rc in range(ts // CONV_ROWS):
        r0 = rc * CONV_ROWS
        rows = slice(r0, r0 + CONV_ROWS)
        for cb in range(ncb):
            cols = slice(cb * LANES, (cb + 1) * LANES)
            for phase in range(2):
                acc = jnp.broadcast_to(dwb_ref[:, cols], (half, LANES))
                for k in range(CONV_WIDTH):
                    acc = acc + dw_ref[k:k + 1, cols] * ypad_ref[cb, pl.ds(r0 + phase + tap0 + k, half, stride=2), :]
                acc_ref[cb, pl.ds(r0 + phase, half, stride=2), :] = acc
        yv = jnp.concatenate([acc_ref[cb, rows, :] for cb in range(ncb)], axis=-1)
        yc = yv - jnp.mean(yv, axis=-1, keepdims=True)
        yn = yc * lax.rsqrt(jnp.mean(yc * yc, axis=-1, keepdims=True) + EPS) * lng_ref[...] + lnb_ref[...]
        z = (_silu(yn) * gate_ref[rows, :]).astype(BF16)
        x1 = x_ref[0, rows, :] + _dot(z, wout_ref[...])
        o_ref[0, rows, :] = _ple_update(x1, p_ref[0, rows, :], png_ref[...], wgate_ref[...], wproj_ref[...])
    ypad_ref[:, 0:CONV_HALO, :] = ypad_ref[:, ts:ts + CONV_HALO, :]


def _conv_layer(x, p, layer, ng, w_in, dw, dw_b, ln_g, ln_b, w_out, png, w_gate, w_proj):
    B, S, D = x.shape
    ts = SEQ_TILE
    row = lambda v: v.reshape(1, -1).astype(F32)
    return pl.pallas_call(
        _conv_layer_kernel,
        grid=(B, S // ts),
        in_specs=[pl.BlockSpec((1, ts, D), lambda b, s: (b, s, 0)),
                  pl.BlockSpec((None, 1, ts, PLE_DIM), lambda b, s: (layer, b, s, 0)),
                  _const_spec((1, D)), _const_spec((D, 3 * CONV_DIM)),
                  _const_spec((CONV_WIDTH, CONV_DIM)), _const_spec((1, CONV_DIM)),
                  _const_spec((1, CONV_DIM)), _const_spec((1, CONV_DIM)),
                  _const_spec((CONV_DIM, D)), _const_spec((1, D)),
                  _const_spec((D, D)), _const_spec((PLE_DIM, D))],
        out_specs=pl.BlockSpec((1, ts, D), lambda b, s: (b, s, 0)),
        out_shape=jax.ShapeDtypeStruct((B, S, D), F32),
        scratch_shapes=[pltpu.VMEM((CONV_DIM // LANES, CONV_HALO + ts, LANES), F32),
                        pltpu.VMEM((CONV_DIM // LANES, ts, LANES), F32),
                        pltpu.VMEM((ts, CONV_DIM), F32)],
        compiler_params=pltpu.CompilerParams(dimension_semantics=("parallel", "arbitrary"),
                                             vmem_limit_bytes=VMEM_LIMIT),
        name="conv_layer",
    )(x, p, row(ng), w_in.astype(BF16), dw.astype(F32), row(dw_b), row(ln_g), row(ln_b),
      w_out.astype(BF16), row(png), w_gate.astype(BF16), w_proj.astype(BF16))


def _attn_in_kernel(x_ref, ng_ref, win_ref, qg_ref, kg_ref, cos_ref, sin_ref, hsum_ref,
                    q0_ref, q1_ref, q2_ref, k0_ref, k1_ref, k2_ref, v0_ref, v1_ref, v2_ref,
                    sg_ref, perm_ref):
    ts = x_ref.shape[1]
    h = _rms_norm(x_ref[0], ng_ref[...]).astype(BF16)
    cos_t = cos_ref[0]
    sin_t = sin_ref[0]
    lane = lax.broadcasted_iota(jnp.int32, (1, LANES), 1) % HEAD_DIM
    first_half = lane < (ROT_DIM // 2)
    outs = ((q0_ref, q1_ref, q2_ref), (k0_ref, k1_ref, k2_ref), (v0_ref, v1_ref, v2_ref))
    gains = (qg_ref, kg_ref)

    def emit(out_ref, g, j, val, slab):
        d = DILATION_GROUPS[g][1]
        cols = slice(j * LANES, (j + 1) * LANES)
        if d == 1:
            out_ref[0, :, cols] = val.astype(BF16)
            return
        perm_ref[slab] = val
        for r in range(d):
            out_ref[0, r, :, cols] = perm_ref[slab, pl.ds(r, ts // d, stride=d), :].astype(BF16)

    for kind in range(3):
        for g in range(N_GROUPS):
            c0 = kind * QKV_DIM + g * GROUP_DIM
            u = _dot(h, win_ref[:, c0:c0 + GROUP_DIM])
            if kind < 2:
                ss = jnp.concatenate(
                    [_dot((uw * uw).astype(BF16), hsum_ref[...])
                     for uw in (u[:, :MXU_COLS], u[:, MXU_COLS:])], axis=-1)
                u = u * lax.rsqrt(ss * (1.0 / HEAD_DIM) + EPS)
            for j in range(GROUP_DIM // LANES):
                ub = u[:, j * LANES:(j + 1) * LANES]
                if kind < 2:
                    un = ub * gains[kind][...]
                    partner = jnp.where(first_half, pltpu.roll(un, LANES - ROT_DIM // 2, 1),
                                        pltpu.roll(un, ROT_DIM // 2, 1))
                    ub = un * cos_t + partner * sin_t
                    if kind == 0:
                        ub = ub * Q_SCALE
                emit(outs[kind][g], g, j, ub, j)
    gate = _dot(h, win_ref[:, 3 * QKV_DIM:3 * QKV_DIM + GROUP_DIM])
    sg_ref[0] = _silu(gate).astype(BF16)


def _attn_in(x, ng, w_in, q_gain, k_gain, cos_t, sin_t):
    B, S, D = x.shape
    ts = SEQ_TILE
    hsum = (jnp.arange(MXU_COLS)[:, None] // HEAD_DIM == jnp.arange(MXU_COLS)[None, :] // HEAD_DIM).astype(BF16)
    lane_gain = lambda v: jnp.tile(v.astype(F32), LANES // HEAD_DIM)[None, :]

    def group_spec(g):
        d = DILATION_GROUPS[g][1]
        if d == 1:
            return pl.BlockSpec((1, ts, GROUP_DIM), lambda b, s: (b, s, 0))
        return pl.BlockSpec((1, d, ts // d, GROUP_DIM), lambda b, s: (b, 0, s, 0))

    def group_shape(g):
        d = DILATION_GROUPS[g][1]
        if d == 1:
            return jax.ShapeDtypeStruct((B, S, GROUP_DIM), BF16)
        return jax.ShapeDtypeStruct((B, d, S // d, GROUP_DIM), BF16)

    w_cols = 3 * QKV_DIM + GROUP_DIM
    return pl.pallas_call(
        _attn_in_kernel,
        grid=(B, S // ts),
        in_specs=[pl.BlockSpec((1, ts, D), lambda b, s: (b, s, 0)),
                  _const_spec((1, D)), _const_spec((D, w_cols)),
                  _const_spec((1, LANES)), _const_spec((1, LANES)),
                  pl.BlockSpec((1, ts, LANES), lambda b, s: (b, s, 0)),
                  pl.BlockSpec((1, ts, LANES), lambda b, s: (b, s, 0)),
                  _const_spec((MXU_COLS, MXU_COLS))],
        out_specs=[group_spec(g) for _ in range(3) for g in range(N_GROUPS)]
        + [pl.BlockSpec((1, ts, GROUP_DIM), lambda b, s: (b, s, 0))],
        out_shape=[group_shape(g) for _ in range(3) for g in range(N_GROUPS)]
        + [jax.ShapeDtypeStruct((B, S, GROUP_DIM), BF16)],
        scratch_shapes=[pltpu.VMEM((GROUP_DIM // LANES, ts, LANES), F32)],
        compiler_params=pltpu.CompilerParams(dimension_semantics=("parallel", "parallel"),
                                             vmem_limit_bytes=VMEM_LIMIT),
        name="attn_in",
    )(x, ng.reshape(1, -1), w_in.astype(BF16), lane_gain(q_gain), lane_gain(k_gain), cos_t, sin_t, hsum)


def _band_attn_kernel(q_ref, k_ref, v_ref, bias_ref, o_ref, lse_ref, kf_ref, vf_ref):
    nres, tq = q_ref.shape[1], q_ref.shape[2]
    i = pl.program_id(2)

    @pl.when(i == 0)
    def _():
        kf_ref[:, 0:BAND, :] = jnp.zeros((nres, BAND, GROUP_DIM), BF16)
        vf_ref[:, 0:BAND, :] = jnp.zeros((nres, BAND, GROUP_DIM), BF16)

    kf_ref[:, BAND:BAND + tq, :] = k_ref[0]
    vf_ref[:, BAND:BAND + tq, :] = v_ref[0]

    lane = lax.broadcasted_iota(jnp.int32, (1, LANES), 1)
    low = lane < HEAD_DIM
    head_mask = (low.astype(BF16), (~low).astype(BF16))
    eye = (lax.broadcasted_iota(jnp.int32, (Q_BLOCK, Q_BLOCK), 0)
           == lax.broadcasted_iota(jnp.int32, (Q_BLOCK, Q_BLOCK), 1)).astype(BF16)
    ones = jnp.ones((2 * BAND, LANES), BF16)

    npair = GROUP_DIM // LANES
    lse_lanes = LANES // npair

    units = [(res, sb * Q_BLOCK, pair) for res in range(nres) for sb in range(tq // Q_BLOCK)
             for pair in range(npair)]

    scores = []
    for res, r0, pair in units:
        cols = slice(pair * LANES, (pair + 1) * LANES)
        qp = q_ref[0, res, r0:r0 + Q_BLOCK, cols]
        kp = kf_ref[res, r0:r0 + 2 * BAND, cols]
        bias_t = bias_ref[(i == 0).astype(jnp.int32)] if r0 == 0 else bias_ref[0]
        q_aug = jnp.concatenate([jnp.concatenate([qp * head_mask[hh], eye], axis=1) for hh in range(2)],
                                axis=0)
        k_aug = jnp.concatenate([kp, bias_t], axis=1)
        scores.append(lax.dot_general(q_aug, k_aug, (((1,), (1,)), ((), ())), preferred_element_type=F32))

    probs, maxes = [], []
    for s in scores:
        m = jnp.max(s, axis=-1, keepdims=True)
        probs.append(jnp.exp2(s - m).astype(BF16))
        maxes.append(m)

    lse_c = None
    for (res, r0, pair), e, m in zip(units, probs, maxes):
        cols = slice(pair * LANES, (pair + 1) * LANES)
        vp = vf_ref[res, r0:r0 + 2 * BAND, cols]
        out = _dot(e, jnp.concatenate([vp, ones], axis=1))
        acc = jnp.where(low, out[:Q_BLOCK, :LANES], out[Q_BLOCK:, :LANES])
        den = jnp.where(low, out[:Q_BLOCK, LANES:], out[Q_BLOCK:, LANES:])
        mx = jnp.where(low, m[:Q_BLOCK], m[Q_BLOCK:])
        o_ref[0, res, r0:r0 + Q_BLOCK, cols] = (acc / den).astype(BF16)
        lse = (mx + jnp.log2(den)) * LN2
        shift = (lse_lanes * pair + lse_lanes // 2 - HEAD_DIM) % LANES
        rolled = pltpu.roll(lse, shift, 1)
        lse_c = rolled if pair == 0 else jnp.where(lane // lse_lanes == pair, rolled, lse_c)
        if pair == npair - 1:
            lse_ref[0, res, r0:r0 + Q_BLOCK, :] = lse_c

    kf_ref[:, 0:BAND, :] = kf_ref[:, tq:tq + BAND, :]
    vf_ref[:, 0:BAND, :] = vf_ref[:, tq:tq + BAND, :]


def _band_bias():
    kj = jnp.arange(2 * BAND)[:, None]
    qi = jnp.arange(Q_BLOCK)[None, :]
    dist = BAND + qi - kj
    band = (dist >= 0) & (dist <= BAND)
    both = jnp.stack([band, band & (kj >= BAND)])
    return jnp.where(both, 0.0, NEG_INF).astype(BF16)


def _band_attn(q, k, v):
    B, d, L, _ = q.shape
    tq = min(BAND_ROWS, L)
    nres = min(d, BAND_ROWS // tq)
    spec = pl.BlockSpec((1, nres, tq, GROUP_DIM), lambda b, r, i: (b, r, i, 0))
    lse_spec = pl.BlockSpec((1, nres, tq, LANES), lambda b, r, i: (b, r, i, 0))
    return pl.pallas_call(
        _band_attn_kernel,
        grid=(B, d // nres, L // tq),
        in_specs=[spec, spec, spec, _const_spec((2, 2 * BAND, Q_BLOCK))],
        out_specs=[spec, lse_spec],
        out_shape=[jax.ShapeDtypeStruct(q.shape, BF16), jax.ShapeDtypeStruct((B, d, L, LANES), F32)],
        scratch_shapes=[pltpu.VMEM((nres, BAND + tq, GROUP_DIM), BF16)] * 2,
        compiler_params=pltpu.CompilerParams(dimension_semantics=("parallel", "parallel", "arbitrary"),
                                             vmem_limit_bytes=VMEM_LIMIT),
        name=f"band_attn_d{d}",
    )(q, k, v, _band_bias())


def _attn_out_kernel(x_ref, p_ref, sg_ref, o0_ref, o1_ref, o2_ref, l0_ref, l1_ref, l2_ref,
                     expand_ref, wout_ref, png_ref, wgate_ref, wproj_ref, out_ref, perm_ref):
    ts = x_ref.shape[1]

    def natural(ref, g, slab0):
        d = DILATION_GROUPS[g][1]
        nblk = ref.shape[-1] // LANES
        if d == 1:
            return ref[0].astype(F32)
        for j in range(nblk):
            for r in range(d):
                perm_ref[slab0 + j, pl.ds(r, ts // d, stride=d), :] = (
                    ref[0, r, :, j * LANES:(j + 1) * LANES].astype(F32))
        return jnp.concatenate([perm_ref[slab0 + j] for j in range(nblk)], axis=-1)

    o_refs = (o0_ref, o1_ref, o2_ref)
    l_refs = (l0_ref, l1_ref, l2_ref)
    o_slabs = GROUP_DIM // LANES
    lses = [natural(l_refs[g], g, N_GROUPS * o_slabs + g) for g in range(N_GROUPS)]
    m = jnp.maximum(jnp.maximum(lses[0], lses[1]), lses[2])
    wts = [jnp.exp(l - m) for l in lses]
    inv = 1.0 / (wts[0] + wts[1] + wts[2])
    o = sum(_dot((wts[g] * inv).astype(BF16), expand_ref[...]) * natural(o_refs[g], g, g * o_slabs)
            for g in range(N_GROUPS))
    y = (o * sg_ref[0].astype(F32)).astype(BF16)
    x1 = x_ref[0] + _dot(y, wout_ref[...])
    out_ref[0] = _ple_update(x1, p_ref[0], png_ref[...], wgate_ref[...], wproj_ref[...])


def _attn_out(x, p, layer, sg, os_, lses, w_out, png, w_gate, w_proj):
    B, S, D = x.shape
    ts = SEQ_TILE

    def group_spec(g, width):
        d = DILATION_GROUPS[g][1]
        if d == 1:
            return pl.BlockSpec((1, ts, width), lambda b, s: (b, s, 0))
        return pl.BlockSpec((1, d, ts // d, width), lambda b, s: (b, 0, s, 0))

    lanes_per_head = LANES // HEADS_PER_GROUP
    expand = (jnp.arange(LANES)[:, None]
              == (jnp.arange(GROUP_DIM)[None, :] // HEAD_DIM) * lanes_per_head + lanes_per_head // 2).astype(BF16)
    return pl.pallas_call(
        _attn_out_kernel,
        grid=(B, S // ts),
        in_specs=[pl.BlockSpec((1, ts, D), lambda b, s: (b, s, 0)),
                  pl.BlockSpec((None, 1, ts, PLE_DIM), lambda b, s: (layer, b, s, 0)),
                  pl.BlockSpec((1, ts, GROUP_DIM), lambda b, s: (b, s, 0))]
        + [group_spec(g, GROUP_DIM) for g in range(N_GROUPS)]
        + [group_spec(g, LANES) for g in range(N_GROUPS)]
        + [_const_spec((LANES, GROUP_DIM)), _const_spec((GROUP_DIM, D)), _const_spec((1, D)),
           _const_spec((D, D)), _const_spec((PLE_DIM, D))],
        out_specs=pl.BlockSpec((1, ts, D), lambda b, s: (b, s, 0)),
        out_shape=jax.ShapeDtypeStruct((B, S, D), F32),
        scratch_shapes=[pltpu.VMEM((N_GROUPS * (GROUP_DIM // LANES + 1), ts, LANES), F32)],
        compiler_params=pltpu.CompilerParams(dimension_semantics=("parallel", "parallel"),
                                             vmem_limit_bytes=VMEM_LIMIT),
        name="attn_out",
    )(x, p, sg, *os_, *lses, expand, w_out.astype(BF16), png.reshape(1, -1), w_gate.astype(BF16),
      w_proj.astype(BF16))


def _attn_layer(x, p, layer, ng, w_in, q_gain, k_gain, w_out, png, w_gate, w_proj, cos_t, sin_t):
    B, S, _ = x.shape
    q0, q1, q2, k0, k1, k2, v0, v1, v2, sg = _attn_in(x, ng, w_in, q_gain, k_gain, cos_t, sin_t)
    os_, lses = [], []
    for q, k, v in ((q0, k0, v0), (q1, k1, v1), (q2, k2, v2)):
        if q.ndim == 3:
            shape = (B, 1, S, GROUP_DIM)
            o, lse = _band_attn(q.reshape(shape), k.reshape(shape), v.reshape(shape))
            o, lse = o.reshape(B, S, GROUP_DIM), lse.reshape(B, S, LANES)
        else:
            o, lse = _band_attn(q, k, v)
        os_.append(o)
        lses.append(lse)
    return _attn_out(x, p, layer, sg, os_, lses, w_out, png, w_gate, w_proj)


def kernel(x, p, positions, norm_g, conv_w_in, conv_dw, conv_dw_b, conv_ln_g, conv_ln_b, conv_w_out,
           attn_w_in, attn_q_norm, attn_k_norm, attn_w_out, ple_w_proj, ple_norm_g, ple_w_gate):
    depth = norm_g.shape[0]
    cos_t, sin_t = _rope_tables(positions)
    for i in range(depth):
        j = i // 2
        if i % 2 == 0:
            x = _conv_layer(x, p, i, norm_g[i], conv_w_in[j], conv_dw[j], conv_dw_b[j], conv_ln_g[j],
                            conv_ln_b[j], conv_w_out[j], ple_norm_g[i], ple_w_gate[i], ple_w_proj[i])
        else:
            x = _attn_layer(x, p, i, norm_g[i], attn_w_in[j], attn_q_norm[j], attn_k_norm[j],
                            attn_w_out[j], ple_norm_g[i], ple_w_gate[i], ple_w_proj[i], cos_t, sin_t)
    return x
```

```python
import functools

import jax
import jax.numpy as jnp
from jax import lax
from jax.experimental import pallas as pl
from jax.experimental.pallas import tpu as pltpu

F32 = jnp.float32
BF16 = jnp.bfloat16

D_MODEL = 1024
PLE_DIM = 256
CONV_DIM = D_MODEL
CONV_WIDTH = 31
HEAD_DIM = 64
HEADS_PER_GROUP = 8
DILATION_GROUPS = ((128, 1), (512, 4), (2048, 16))
N_GROUPS = len(DILATION_GROUPS)
GROUP_DIM = HEADS_PER_GROUP * HEAD_DIM
QKV_DIM = N_GROUPS * GROUP_DIM
BAND = 128
ROPE_THETA = 500000.0
ROT_DIM = HEAD_DIM // 4
EPS = 1e-6
NEG_INF = -1e30

LANES = 128
MXU_COLS = 256
LOG2E = 1.4426950408889634
LN2 = 0.6931471805599453
Q_SCALE = HEAD_DIM ** -0.5 * LOG2E
CONV_HALO = 32
VMEM_LIMIT = 56 * 1024 * 1024

SEQ_TILE = 512
OUT_TILE = 1024
CONV_ROWS = 128
PROJ_ROWS = 512
TAIL_ROWS = 256
Q_BLOCK = 128
BAND_ROWS = 2048


def _const_spec(shape):
    return pl.BlockSpec(shape, lambda *_: (0,) * len(shape), pipeline_mode=pl.Buffered(1))


def _dot(a, b):
    return jnp.dot(a, b, preferred_element_type=F32)


def _rms_norm(x, g):
    return x * lax.rsqrt(jnp.mean(x * x, axis=-1, keepdims=True) + EPS) * g


def _silu(x):
    return x * jax.nn.sigmoid(x)


def _ple_update(x1, p, png, w_gate, w_proj):
    gate = jax.nn.sigmoid(_dot(_rms_norm(x1, png).astype(BF16), w_gate))
    return x1 + _dot(p.astype(BF16), w_proj) * gate


def _rope_table_kernel(pos_ref, freq_ref, sign_ref, cos_ref, sin_ref):
    ang = pos_ref[0].astype(F32) * freq_ref[...]
    cos_ref[0] = jnp.cos(ang)
    sin_ref[0] = jnp.sin(ang) * sign_ref[...]


def _rope_tables(positions):
    B, S = positions.shape
    lane = jnp.arange(LANES) % HEAD_DIM
    half = ROT_DIM // 2
    inv_freq = 1.0 / (ROPE_THETA ** (jnp.arange(0, ROT_DIM, 2, dtype=F32) / ROT_DIM))
    freq = jnp.where(lane < ROT_DIM, inv_freq[lane % half], 0.0).astype(F32)[None, :]
    sign = jnp.where(lane < half, -1.0, jnp.where(lane < ROT_DIM, 1.0, 0.0)).astype(F32)[None, :]
    ts = SEQ_TILE
    return pl.pallas_call(
        _rope_table_kernel,
        grid=(B, S // ts),
        in_specs=[pl.BlockSpec((1, ts, 1), lambda b, s: (b, s, 0)),
                  _const_spec((1, LANES)), _const_spec((1, LANES))],
        out_specs=[pl.BlockSpec((1, ts, LANES), lambda b, s: (b, s, 0))] * 2,
        out_shape=[jax.ShapeDtypeStruct((B, S, LANES), F32)] * 2,
        compiler_params=pltpu.CompilerParams(dimension_semantics=("parallel", "parallel")),
        name="rope_tables",
    )(positions.reshape(B, S, 1), freq, sign)


def _conv_layer_kernel(x_ref, p_ref, ng_ref, win_ref, dw_ref, dwb_ref, lng_ref, lnb_ref, wout_ref,
                       png_ref, wgate_ref, wproj_ref, o_ref, ypad_ref, acc_ref, gate_ref):
    ts = x_ref.shape[1]
    c = CONV_DIM
    ncb = c // LANES
    half = CONV_ROWS // 2

    @pl.when(pl.program_id(1) == 0)
    def _():
        ypad_ref[:, 0:CONV_HALO, :] = jnp.zeros((ncb, CONV_HALO, LANES), F32)

    tap0 = CONV_HALO - (CONV_WIDTH - 1)

    for p0 in range(0, ts, PROJ_ROWS):
        prow = slice(p0, p0 + PROJ_ROWS)
        h = _rms_norm(x_ref[0, prow, :], ng_ref[...]).astype(BF16)
        y = _dot(h, win_ref[:, 0:c]) * jax.nn.sigmoid(_dot(h, win_ref[:, c:2 * c]))
        for cb in range(ncb):
            ypad_ref[cb, CONV_HALO + p0:CONV_HALO + p0 + PROJ_ROWS, :] = y[:, cb * LANES:(cb + 1) * LANES]
        gate_ref[prow, :] = _silu(_dot(h, win_ref[:, 2 * c:3 * c]))

    for t0 in range(0, ts, TAIL_ROWS):
        rows = slice(t0, t0 + TAIL_ROWS)
        for r0 in range(t0, t0 + TAIL_ROWS, CONV_ROWS):
            for cb in range(ncb):
                cols = slice(cb * LANES, (cb + 1) * LANES)
                for phase in range(2):
                    acc = jnp.broadcast_to(dwb_ref[:, cols], (half, LANES))
                    for k in range(CONV_WIDTH):
                        acc = acc + dw_ref[k:k + 1, cols] * ypad_ref[cb, pl.ds(r0 + phase + tap0 + k, half, stride=2), :]
                    acc_ref[cb, pl.ds(r0 + phase, half, stride=2), :] = acc
        yv = jnp.concatenate([acc_ref[cb, rows, :] for cb in range(ncb)], axis=-1)
        yc = yv - jnp.mean(yv, axis=-1, keepdims=True)
        yn = yc * lax.rsqrt(jnp.mean(yc * yc, axis=-1, keepdims=True) + EPS) * lng_ref[...] + lnb_ref[...]
        z = (_silu(yn) * gate_ref[rows, :]).astype(BF16)
        x1 = x_ref[0, rows, :] + _dot(z, wout_ref[...])
        o_ref[0, rows, :] = _ple_update(x1, p_ref[0, rows, :], png_ref[...], wgate_ref[...], wproj_ref[...])
    ypad_ref[:, 0:CONV_HALO, :] = ypad_ref[:, ts:ts + CONV_HALO, :]


def _conv_layer(x, p, layer, ng, w_in, dw, dw_b, ln_g, ln_b, w_out, png, w_gate, w_proj):
    B, S, D = x.shape
    ts = SEQ_TILE
    row = lambda v: v.reshape(1, -1).astype(F32)
    return pl.pallas_call(
        _conv_layer_kernel,
        grid=(B, S // ts),
        in_specs=[pl.BlockSpec((1, ts, D), lambda b, s: (b, s, 0)),
                  pl.BlockSpec((None, 1, ts, PLE_DIM), lambda b, s: (layer, b, s, 0)),
                  _const_spec((1, D)), _const_spec((D, 3 * CONV_DIM)),
                  _const_spec((CONV_WIDTH, CONV_DIM)), _const_spec((1, CONV_DIM)),
                  _const_spec((1, CONV_DIM)), _const_spec((1, CONV_DIM)),
                  _const_spec((CONV_DIM, D)), _const_spec((1, D)),
                  _const_spec((D, D)), _const_spec((PLE_DIM, D))],
        out_specs=pl.BlockSpec((1, ts, D), lambda b, s: (b, s, 0)),
        out_shape=jax.ShapeDtypeStruct((B, S, D), F32),
        scratch_shapes=[pltpu.VMEM((CONV_DIM // LANES, CONV_HALO + ts, LANES), F32),
                        pltpu.VMEM((CONV_DIM // LANES, ts, LANES), F32),
                        pltpu.VMEM((ts, CONV_DIM), F32)],
        compiler_params=pltpu.CompilerParams(dimension_semantics=("parallel", "arbitrary"),
                                             vmem_limit_bytes=VMEM_LIMIT),
        name="conv_layer",
    )(x, p, row(ng), w_in.astype(BF16), dw.astype(F32), row(dw_b), row(ln_g), row(ln_b),
      w_out.astype(BF16), row(png), w_gate.astype(BF16), w_proj.astype(BF16))


def _attn_in_kernel(x_ref, ng_ref, win_ref, qg_ref, kg_ref, cos_ref, sin_ref, hsum_ref,
                    q0_ref, q1_ref, q2_ref, k0_ref, k1_ref, k2_ref, v0_ref, v1_ref, v2_ref,
                    sg_ref, perm_ref):
    ts = x_ref.shape[1]
    h = _rms_norm(x_ref[0], ng_ref[...]).astype(BF16)
    cos_t = cos_ref[0]
    sin_t = sin_ref[0]
    lane = lax.broadcasted_iota(jnp.int32, (1, LANES), 1) % HEAD_DIM
    first_half = lane < (ROT_DIM // 2)
    outs = ((q0_ref, q1_ref, q2_ref), (k0_ref, k1_ref, k2_ref), (v0_ref, v1_ref, v2_ref))
    gains = (qg_ref, kg_ref)

    def emit(out_ref, g, j, val, slab):
        d = DILATION_GROUPS[g][1]
        cols = slice(j * LANES, (j + 1) * LANES)
        if d == 1:
            out_ref[0, :, cols] = val.astype(BF16)
            return
        perm_ref[slab] = val
        for r in range(d):
            out_ref[0, r, :, cols] = perm_ref[slab, pl.ds(r, ts // d, stride=d), :].astype(BF16)

    for kind in range(3):
        for g in range(N_GROUPS):
            c0 = kind * QKV_DIM + g * GROUP_DIM
            u = _dot(h, win_ref[:, c0:c0 + GROUP_DIM])
            if kind < 2:
                ss = jnp.concatenate(
                    [_dot((uw * uw).astype(BF16), hsum_ref[...])
                     for uw in (u[:, :MXU_COLS], u[:, MXU_COLS:])], axis=-1)
                u = u * lax.rsqrt(ss * (1.0 / HEAD_DIM) + EPS)
            for j in range(GROUP_DIM // LANES):
                ub = u[:, j * LANES:(j + 1) * LANES]
                if kind < 2:
                    un = ub * gains[kind][...]
                    partner = jnp.where(first_half, pltpu.roll(un, LANES - ROT_DIM // 2, 1),
                                        pltpu.roll(un, ROT_DIM // 2, 1))
                    ub = un * cos_t + partner * sin_t
                    if kind == 0:
                        ub = ub * Q_SCALE
                emit(outs[kind][g], g, j, ub, j)
    gate = _dot(h, win_ref[:, 3 * QKV_DIM:3 * QKV_DIM + GROUP_DIM])
    sg_ref[0] = _silu(gate).astype(BF16)


def _attn_in(x, ng, w_in, q_gain, k_gain, cos_t, sin_t):
    B, S, D = x.shape
    ts = SEQ_TILE
    hsum = (jnp.arange(MXU_COLS)[:, None] // HEAD_DIM == jnp.arange(MXU_COLS)[None, :] // HEAD_DIM).astype(BF16)
    lane_gain = lambda v: jnp.tile(v.astype(F32), LANES // HEAD_DIM)[None, :]

    def group_spec(g):
        d = DILATION_GROUPS[g][1]
        if d == 1:
            return pl.BlockSpec((1, ts, GROUP_DIM), lambda b, s: (b, s, 0))
        return pl.BlockSpec((1, d, ts // d, GROUP_DIM), lambda b, s: (b, 0, s, 0))

    def group_shape(g):
        d = DILATION_GROUPS[g][1]
        if d == 1:
            return jax.ShapeDtypeStruct((B, S, GROUP_DIM), BF16)
        return jax.ShapeDtypeStruct((B, d, S // d, GROUP_DIM), BF16)

    w_cols = 3 * QKV_DIM + GROUP_DIM
    return pl.pallas_call(
        _attn_in_kernel,
        grid=(B, S // ts),
        in_specs=[pl.BlockSpec((1, ts, D), lambda b, s: (b, s, 0)),
                  _const_spec((1, D)), _const_spec((D, w_cols)),
                  _const_spec((1, LANES)), _const_spec((1, LANES)),
                  pl.BlockSpec((1, ts, LANES), lambda b, s: (b, s, 0)),
                  pl.BlockSpec((1, ts, LANES), lambda b, s: (b, s, 0)),
                  _const_spec((MXU_COLS, MXU_COLS))],
        out_specs=[group_spec(g) for _ in range(3) for g in range(N_GROUPS)]
        + [pl.BlockSpec((1, ts, GROUP_DIM), lambda b, s: (b, s, 0))],
        out_shape=[group_shape(g) for _ in range(3) for g in range(N_GROUPS)]
        + [jax.ShapeDtypeStruct((B, S, GROUP_DIM), BF16)],
        scratch_shapes=[pltpu.VMEM((GROUP_DIM // LANES, ts, LANES), F32)],
        compiler_params=pltpu.CompilerParams(dimension_semantics=("parallel", "parallel"),
                                             vmem_limit_bytes=VMEM_LIMIT),
        name="attn_in",
    )(x, ng.reshape(1, -1), w_in.astype(BF16), lane_gain(q_gain), lane_gain(k_gain), cos_t, sin_t, hsum)


def _band_attn_kernel(q_ref, k_ref, v_ref, bias_ref, o_ref, lse_ref, kf_ref, vf_ref):
    nres, tq = q_ref.shape[1], q_ref.shape[2]
    i = pl.program_id(2)

    @pl.when(i == 0)
    def _():
        kf_ref[:, 0:BAND, :] = jnp.zeros((nres, BAND, GROUP_DIM), BF16)
        vf_ref[:, 0:BAND, :] = jnp.zeros((nres, BAND, GROUP_DIM), BF16)

    kf_ref[:, BAND:BAND + tq, :] = k_ref[0]
    vf_ref[:, BAND:BAND + tq, :] = v_ref[0]

    lane = lax.broadcasted_iota(jnp.int32, (1, LANES), 1)
    low = lane < HEAD_DIM
    head_mask = (low.astype(BF16), (~low).astype(BF16))
    eye = (lax.broadcasted_iota(jnp.int32, (Q_BLOCK, Q_BLOCK), 0)
           == lax.broadcasted_iota(jnp.int32, (Q_BLOCK, Q_BLOCK), 1)).astype(BF16)
    ones = jnp.ones((2 * BAND, LANES), BF16)

    npair = GROUP_DIM // LANES
    lse_lanes = LANES // npair

    units = [(res, sb * Q_BLOCK, pair) for res in range(nres) for sb in range(tq // Q_BLOCK)
             for pair in range(npair)]

    scores = []
    for res, r0, pair in units:
        cols = slice(pair * LANES, (pair + 1) * LANES)
        qp = q_ref[0, res, r0:r0 + Q_BLOCK, cols]
        kp = kf_ref[res, r0:r0 + 2 * BAND, cols]
        bias_t = bias_ref[(i == 0).astype(jnp.int32)] if r0 == 0 else bias_ref[0]
        q_aug = jnp.concatenate([jnp.concatenate([qp * head_mask[hh], eye], axis=1) for hh in range(2)],
                                axis=0)
        k_aug = jnp.concatenate([kp, bias_t], axis=1)
        scores.append(lax.dot_general(q_aug, k_aug, (((1,), (1,)), ((), ())), preferred_element_type=F32))

    probs, maxes = [], []
    for s in scores:
        m = jnp.max(s, axis=-1, keepdims=True)
        probs.append(jnp.exp2(s - m).astype(BF16))
        maxes.append(m)

    lse_c = None
    for (res, r0, pair), e, m in zip(units, probs, maxes):
        cols = slice(pair * LANES, (pair + 1) * LANES)
        vp = vf_ref[res, r0:r0 + 2 * BAND, cols]
        out = _dot(e, jnp.concatenate([vp, ones], axis=1))
        acc = jnp.where(low, out[:Q_BLOCK, :LANES], out[Q_BLOCK:, :LANES])
        den = jnp.where(low, out[:Q_BLOCK, LANES:], out[Q_BLOCK:, LANES:])
        mx = jnp.where(low, m[:Q_BLOCK], m[Q_BLOCK:])
        o_ref[0, res, r0:r0 + Q_BLOCK, cols] = (acc / den).astype(BF16)
        lse = (mx + jnp.log2(den)) * LN2
        shift = (lse_lanes * pair + lse_lanes // 2 - HEAD_DIM) % LANES
        rolled = pltpu.roll(lse, shift, 1)
        lse_c = rolled if pair == 0 else jnp.where(lane // lse_lanes == pair, rolled, lse_c)
        if pair == npair - 1:
            lse_ref[0, res, r0:r0 + Q_BLOCK, :] = lse_c

    kf_ref[:, 0:BAND, :] = kf_ref[:, tq:tq + BAND, :]
    vf_ref[:, 0:BAND, :] = vf_ref[:, tq:tq + BAND, :]


def _band_bias():
    kj = jnp.arange(2 * BAND)[:, None]
    qi = jnp.arange(Q_BLOCK)[None, :]
    dist = BAND + qi - kj
    band = (dist >= 0) & (dist <= BAND)
    both = jnp.stack([band, band & (kj >= BAND)])
    return jnp.where(both, 0.0, NEG_INF).astype(BF16)


def _band_attn(q, k, v):
    B, d, L, _ = q.shape
    tq = min(BAND_ROWS, L)
    nres = min(d, BAND_ROWS // tq)
    spec = pl.BlockSpec((1, nres, tq, GROUP_DIM), lambda b, r, i: (b, r, i, 0))
    lse_spec = pl.BlockSpec((1, nres, tq, LANES), lambda b, r, i: (b, r, i, 0))
    return pl.pallas_call(
        _band_attn_kernel,
        grid=(B, d // nres, L // tq),
        in_specs=[spec, spec, spec, _const_spec((2, 2 * BAND, Q_BLOCK))],
        out_specs=[spec, lse_spec],
        out_shape=[jax.ShapeDtypeStruct(q.shape, BF16), jax.ShapeDtypeStruct((B, d, L, LANES), F32)],
        scratch_shapes=[pltpu.VMEM((nres, BAND + tq, GROUP_DIM), BF16)] * 2,
        compiler_params=pltpu.CompilerParams(dimension_semantics=("parallel", "parallel", "arbitrary"),
                                             vmem_limit_bytes=VMEM_LIMIT),
        name=f"band_attn_d{d}",
    )(q, k, v, _band_bias())


def _attn_out_kernel(x_ref, p_ref, sg_ref, o0_ref, o1_ref, o2_ref, l0_ref, l1_ref, l2_ref,
                     expand_ref, wout_ref, png_ref, wgate_ref, wproj_ref, out_ref, perm_ref):
    ts = x_ref.shape[1]

    def natural(ref, g, slab0):
        d = DILATION_GROUPS[g][1]
        nblk = ref.shape[-1] // LANES
        if d == 1:
            return ref[0].astype(F32)
        for j in range(nblk):
            for r in range(d):
                perm_ref[slab0 + j, pl.ds(r, ts // d, stride=d), :] = (
                    ref[0, r, :, j * LANES:(j + 1) * LANES].astype(F32))
        return jnp.concatenate([perm_ref[slab0 + j] for j in range(nblk)], axis=-1)

    o_refs = (o0_ref, o1_ref, o2_ref)
    l_refs = (l0_ref, l1_ref, l2_ref)
    o_slabs = GROUP_DIM // LANES
    lses = [natural(l_refs[g], g, N_GROUPS * o_slabs + g) for g in range(N_GROUPS)]
    m = jnp.maximum(jnp.maximum(lses[0], lses[1]), lses[2])
    wts = [jnp.exp(l - m) for l in lses]
    inv = 1.0 / (wts[0] + wts[1] + wts[2])
    o = sum(_dot((wts[g] * inv).astype(BF16), expand_ref[...]) * natural(o_refs[g], g, g * o_slabs)
            for g in range(N_GROUPS))
    y = (o * sg_ref[0].astype(F32)).astype(BF16)
    x1 = x_ref[0] + _dot(y, wout_ref[...])
    out_ref[0] = _ple_update(x1, p_ref[0], png_ref[...], wgate_ref[...], wproj_ref[...])


def _attn_out(x, p, layer, sg, os_, lses, w_out, png, w_gate, w_proj):
    B, S, D = x.shape
    ts = OUT_TILE

    def group_spec(g, width):
        d = DILATION_GROUPS[g][1]
        if d == 1:
            return pl.BlockSpec((1, ts, width), lambda b, s: (b, s, 0))
        return pl.BlockSpec((1, d, ts // d, width), lambda b, s: (b, 0, s, 0))

    lanes_per_head = LANES // HEADS_PER_GROUP
    expand = (jnp.arange(LANES)[:, None]
              == (jnp.arange(GROUP_DIM)[None, :] // HEAD_DIM) * lanes_per_head + lanes_per_head // 2).astype(BF16)
    return pl.pallas_call(
        _attn_out_kernel,
        grid=(B, S // ts),
        in_specs=[pl.BlockSpec((1, ts, D), lambda b, s: (b, s, 0)),
                  pl.BlockSpec((None, 1, ts, PLE_DIM), lambda b, s: (layer, b, s, 0)),
                  pl.BlockSpec((1, ts, GROUP_DIM), lambda b, s: (b, s, 0))]
        + [group_spec(g, GROUP_DIM) for g in range(N_GROUPS)]
        + [group_spec(g, LANES) for g in range(N_GROUPS)]
        + [_const_spec((LANES, GROUP_DIM)), _const_spec((GROUP_DIM, D)), _const_spec((1, D)),
           _const_spec((D, D)), _const_spec((PLE_DIM, D))],
        out_specs=pl.BlockSpec((1, ts, D), lambda b, s: (b, s, 0)),
        out_shape=jax.ShapeDtypeStruct((B, S, D), F32),
        scratch_shapes=[pltpu.VMEM((N_GROUPS * (GROUP_DIM // LANES + 1), ts, LANES), F32)],
        compiler_params=pltpu.CompilerParams(dimension_semantics=("parallel", "parallel"),
                                             vmem_limit_bytes=VMEM_LIMIT),
        name="attn_out",
    )(x, p, sg, *os_, *lses, expand, w_out.astype(BF16), png.reshape(1, -1), w_gate.astype(BF16),
      w_proj.astype(BF16))


def _attn_layer(x, p, layer, ng, w_in, q_gain, k_gain, w_out, png, w_gate, w_proj, cos_t, sin_t):
    B, S, _ = x.shape
    q0, q1, q2, k0, k1, k2, v0, v1, v2, sg = _attn_in(x, ng, w_in, q_gain, k_gain, cos_t, sin_t)
    os_, lses = [], []
    for q, k, v in ((q0, k0, v0), (q1, k1, v1), (q2, k2, v2)):
        if q.ndim == 3:
            shape = (B, 1, S, GROUP_DIM)
            o, lse = _band_attn(q.reshape(shape), k.reshape(shape), v.reshape(shape))
            o, lse = o.reshape(B, S, GROUP_DIM), lse.reshape(B, S, LANES)
        else:
            o, lse = _band_attn(q, k, v)
        os_.append(o)
        lses.append(lse)
    return _attn_out(x, p, layer, sg, os_, lses, w_out, png, w_gate, w_proj)


def kernel(x, p, positions, norm_g, conv_w_in, conv_dw, conv_dw_b, conv_ln_g, conv_ln_b, conv_w_out,
           attn_w_in, attn_q_norm, attn_k_norm, attn_w_out, ple_w_proj, ple_norm_g, ple_w_gate):
    depth = norm_g.shape[0]
    cos_t, sin_t = _rope_tables(positions)
    for i in range(depth):
        j = i // 2
        if i % 2 == 0:
            x = _conv_layer(x, p, i, norm_g[i], conv_w_in[j], conv_dw[j], conv_dw_b[j], conv_ln_g[j],
                            conv_ln_b[j], conv_w_out[j], ple_norm_g[i], ple_w_gate[i], ple_w_proj[i])
        else:
            x = _attn_layer(x, p, i, norm_g[i], attn_w_in[j], attn_q_norm[j], attn_k_norm[j],
                            attn_w_out[j], ple_norm_g[i], ple_w_gate[i], ple_w_proj[i], cos_t, sin_t)
    return x
```

```python
import functools

import jax
import jax.numpy as jnp
from jax import lax
from jax.experimental import pallas as pl
from jax.experimental.pallas import tpu as pltpu

F32 = jnp.float32
BF16 = jnp.bfloat16

D_MODEL = 1024
PLE_DIM = 256
CONV_DIM = D_MODEL
CONV_WIDTH = 31
HEAD_DIM = 64
HEADS_PER_GROUP = 8
DILATION_GROUPS = ((128, 1), (512, 4), (2048, 16))
N_GROUPS = len(DILATION_GROUPS)
GROUP_DIM = HEADS_PER_GROUP * HEAD_DIM
QKV_DIM = N_GROUPS * GROUP_DIM
BAND = 128
ROPE_THETA = 500000.0
ROT_DIM = HEAD_DIM // 4
EPS = 1e-6
NEG_INF = -1e30

LANES = 128
MXU_COLS = 256
LOG2E = 1.4426950408889634
LN2 = 0.6931471805599453
Q_SCALE = HEAD_DIM ** -0.5 * LOG2E
CONV_HALO = 32
VMEM_LIMIT = 56 * 1024 * 1024

SEQ_TILE = 1024
OUT_TILE = 1024
CONV_ROWS = 128
PROJ_ROWS = 512
TAIL_ROWS = 256
Q_BLOCK = 128
BAND_ROWS = 2048


def _const_spec(shape):
    return pl.BlockSpec(shape, lambda *_: (0,) * len(shape), pipeline_mode=pl.Buffered(1))


def _dot(a, b):
    return jnp.dot(a, b, preferred_element_type=F32)


def _rms_norm(x, g):
    return x * lax.rsqrt(jnp.mean(x * x, axis=-1, keepdims=True) + EPS) * g


def _silu(x):
    return x * jax.nn.sigmoid(x)


def _ple_update(x1, p, png, w_gate, w_proj):
    gate = jax.nn.sigmoid(_dot(_rms_norm(x1, png).astype(BF16), w_gate))
    return x1 + _dot(p.astype(BF16), w_proj) * gate


def _rope_table_kernel(pos_ref, freq_ref, sign_ref, cos_ref, sin_ref):
    ang = pos_ref[0].astype(F32) * freq_ref[...]
    cos_ref[0] = jnp.cos(ang)
    sin_ref[0] = jnp.sin(ang) * sign_ref[...]


def _rope_tables(positions):
    B, S = positions.shape
    lane = jnp.arange(LANES) % HEAD_DIM
    half = ROT_DIM // 2
    inv_freq = 1.0 / (ROPE_THETA ** (jnp.arange(0, ROT_DIM, 2, dtype=F32) / ROT_DIM))
    freq = jnp.where(lane < ROT_DIM, inv_freq[lane % half], 0.0).astype(F32)[None, :]
    sign = jnp.where(lane < half, -1.0, jnp.where(lane < ROT_DIM, 1.0, 0.0)).astype(F32)[None, :]
    ts = SEQ_TILE
    return pl.pallas_call(
        _rope_table_kernel,
        grid=(B, S // ts),
        in_specs=[pl.BlockSpec((1, ts, 1), lambda b, s: (b, s, 0)),
                  _const_spec((1, LANES)), _const_spec((1, LANES))],
        out_specs=[pl.BlockSpec((1, ts, LANES), lambda b, s: (b, s, 0))] * 2,
        out_shape=[jax.ShapeDtypeStruct((B, S, LANES), F32)] * 2,
        compiler_params=pltpu.CompilerParams(dimension_semantics=("parallel", "parallel")),
        name="rope_tables",
    )(positions.reshape(B, S, 1), freq, sign)


def _conv_layer_kernel(x_ref, p_ref, ng_ref, win_ref, dw_ref, dwb_ref, lng_ref, lnb_ref, wout_ref,
                       png_ref, wgate_ref, wproj_ref, o_ref, ypad_ref, acc_ref, gate_ref):
    ts = x_ref.shape[1]
    c = CONV_DIM
    ncb = c // LANES
    half = CONV_ROWS // 2

    @pl.when(pl.program_id(1) == 0)
    def _():
        ypad_ref[:, 0:CONV_HALO, :] = jnp.zeros((ncb, CONV_HALO, LANES), F32)

    tap0 = CONV_HALO - (CONV_WIDTH - 1)

    for p0 in range(0, ts, PROJ_ROWS):
        prow = slice(p0, p0 + PROJ_ROWS)
        h = _rms_norm(x_ref[0, prow, :], ng_ref[...]).astype(BF16)
        y = _dot(h, win_ref[:, 0:c]) * jax.nn.sigmoid(_dot(h, win_ref[:, c:2 * c]))
        for cb in range(ncb):
            ypad_ref[cb, CONV_HALO + p0:CONV_HALO + p0 + PROJ_ROWS, :] = y[:, cb * LANES:(cb + 1) * LANES]
        gate_ref[prow, :] = _silu(_dot(h, win_ref[:, 2 * c:3 * c]))

    for t0 in range(0, ts, TAIL_ROWS):
        rows = slice(t0, t0 + TAIL_ROWS)
        for r0 in range(t0, t0 + TAIL_ROWS, CONV_ROWS):
            for cb in range(ncb):
                cols = slice(cb * LANES, (cb + 1) * LANES)
                for phase in range(2):
                    acc = jnp.broadcast_to(dwb_ref[:, cols], (half, LANES))
                    for k in range(CONV_WIDTH):
                        acc = acc + dw_ref[k:k + 1, cols] * ypad_ref[cb, pl.ds(r0 + phase + tap0 + k, half, stride=2), :]
                    acc_ref[cb, pl.ds(r0 + phase, half, stride=2), :] = acc
        yv = jnp.concatenate([acc_ref[cb, rows, :] for cb in range(ncb)], axis=-1)
        yc = yv - jnp.mean(yv, axis=-1, keepdims=True)
        yn = yc * lax.rsqrt(jnp.mean(yc * yc, axis=-1, keepdims=True) + EPS) * lng_ref[...] + lnb_ref[...]
        z = (_silu(yn) * gate_ref[rows, :]).astype(BF16)
        x1 = x_ref[0, rows, :] + _dot(z, wout_ref[...])
        o_ref[0, rows, :] = _ple_update(x1, p_ref[0, rows, :], png_ref[...], wgate_ref[...], wproj_ref[...])
    ypad_ref[:, 0:CONV_HALO, :] = ypad_ref[:, ts:ts + CONV_HALO, :]


def _conv_layer(x, p, layer, ng, w_in, dw, dw_b, ln_g, ln_b, w_out, png, w_gate, w_proj):
    B, S, D = x.shape
    ts = SEQ_TILE
    row = lambda v: v.reshape(1, -1).astype(F32)
    return pl.pallas_call(
        _conv_layer_kernel,
        grid=(B, S // ts),
        in_specs=[pl.BlockSpec((1, ts, D), lambda b, s: (b, s, 0)),
                  pl.BlockSpec((None, 1, ts, PLE_DIM), lambda b, s: (layer, b, s, 0)),
                  _const_spec((1, D)), _const_spec((D, 3 * CONV_DIM)),
                  _const_spec((CONV_WIDTH, CONV_DIM)), _const_spec((1, CONV_DIM)),
                  _const_spec((1, CONV_DIM)), _const_spec((1, CONV_DIM)),
                  _const_spec((CONV_DIM, D)), _const_spec((1, D)),
                  _const_spec((D, D)), _const_spec((PLE_DIM, D))],
        out_specs=pl.BlockSpec((1, ts, D), lambda b, s: (b, s, 0)),
        out_shape=jax.ShapeDtypeStruct((B, S, D), F32),
        scratch_shapes=[pltpu.VMEM((CONV_DIM // LANES, CONV_HALO + ts, LANES), F32),
                        pltpu.VMEM((CONV_DIM // LANES, ts, LANES), F32),
                        pltpu.VMEM((ts, CONV_DIM), F32)],
        compiler_params=pltpu.CompilerParams(dimension_semantics=("parallel", "arbitrary"),
                                             vmem_limit_bytes=VMEM_LIMIT),
        name="conv_layer",
    )(x, p, row(ng), w_in.astype(BF16), dw.astype(F32), row(dw_b), row(ln_g), row(ln_b),
      w_out.astype(BF16), row(png), w_gate.astype(BF16), w_proj.astype(BF16))


def _attn_in_kernel(x_ref, ng_ref, win_ref, qg_ref, kg_ref, cos_ref, sin_ref, hsum_ref,
                    q0_ref, q1_ref, q2_ref, k0_ref, k1_ref, k2_ref, v0_ref, v1_ref, v2_ref,
                    sg_ref, perm_ref):
    ts = x_ref.shape[1]
    h = _rms_norm(x_ref[0], ng_ref[...]).astype(BF16)
    cos_t = cos_ref[0]
    sin_t = sin_ref[0]
    lane = lax.broadcasted_iota(jnp.int32, (1, LANES), 1) % HEAD_DIM
    first_half = lane < (ROT_DIM // 2)
    outs = ((q0_ref, q1_ref, q2_ref), (k0_ref, k1_ref, k2_ref), (v0_ref, v1_ref, v2_ref))
    gains = (qg_ref, kg_ref)

    def emit(out_ref, g, j, val, slab):
        d = DILATION_GROUPS[g][1]
        cols = slice(j * LANES, (j + 1) * LANES)
        if d == 1:
            out_ref[0, :, cols] = val.astype(BF16)
            return
        perm_ref[slab] = val
        for r in range(d):
            out_ref[0, r, :, cols] = perm_ref[slab, pl.ds(r, ts // d, stride=d), :].astype(BF16)

    for kind in range(3):
        for g in range(N_GROUPS):
            c0 = kind * QKV_DIM + g * GROUP_DIM
            u = _dot(h, win_ref[:, c0:c0 + GROUP_DIM])
            if kind < 2:
                ss = jnp.concatenate(
                    [_dot((uw * uw).astype(BF16), hsum_ref[...])
                     for uw in (u[:, :MXU_COLS], u[:, MXU_COLS:])], axis=-1)
                u = u * lax.rsqrt(ss * (1.0 / HEAD_DIM) + EPS)
            for j in range(GROUP_DIM // LANES):
                ub = u[:, j * LANES:(j + 1) * LANES]
                if kind < 2:
                    un = ub * gains[kind][...]
                    partner = jnp.where(first_half, pltpu.roll(un, LANES - ROT_DIM // 2, 1),
                                        pltpu.roll(un, ROT_DIM // 2, 1))
                    ub = un * cos_t + partner * sin_t
                    if kind == 0:
                        ub = ub * Q_SCALE
                emit(outs[kind][g], g, j, ub, j)
    gate = _dot(h, win_ref[:, 3 * QKV_DIM:3 * QKV_DIM + GROUP_DIM])
    sg_ref[0] = _silu(gate).astype(BF16)


def _attn_in(x, ng, w_in, q_gain, k_gain, cos_t, sin_t):
    B, S, D = x.shape
    ts = SEQ_TILE
    hsum = (jnp.arange(MXU_COLS)[:, None] // HEAD_DIM == jnp.arange(MXU_COLS)[None, :] // HEAD_DIM).astype(BF16)
    lane_gain = lambda v: jnp.tile(v.astype(F32), LANES // HEAD_DIM)[None, :]

    def group_spec(g):
        d = DILATION_GROUPS[g][1]
        if d == 1:
            return pl.BlockSpec((1, ts, GROUP_DIM), lambda b, s: (b, s, 0))
        return pl.BlockSpec((1, d, ts // d, GROUP_DIM), lambda b, s: (b, 0, s, 0))

    def group_shape(g):
        d = DILATION_GROUPS[g][1]
        if d == 1:
            return jax.ShapeDtypeStruct((B, S, GROUP_DIM), BF16)
        return jax.ShapeDtypeStruct((B, d, S // d, GROUP_DIM), BF16)

    w_cols = 3 * QKV_DIM + GROUP_DIM
    return pl.pallas_call(
        _attn_in_kernel,
        grid=(B, S // ts),
        in_specs=[pl.BlockSpec((1, ts, D), lambda b, s: (b, s, 0)),
                  _const_spec((1, D)), _const_spec((D, w_cols)),
                  _const_spec((1, LANES)), _const_spec((1, LANES)),
                  pl.BlockSpec((1, ts, LANES), lambda b, s: (b, s, 0)),
                  pl.BlockSpec((1, ts, LANES), lambda b, s: (b, s, 0)),
                  _const_spec((MXU_COLS, MXU_COLS))],
        out_specs=[group_spec(g) for _ in range(3) for g in range(N_GROUPS)]
        + [pl.BlockSpec((1, ts, GROUP_DIM), lambda b, s: (b, s, 0))],
        out_shape=[group_shape(g) for _ in range(3) for g in range(N_GROUPS)]
        + [jax.ShapeDtypeStruct((B, S, GROUP_DIM), BF16)],
        scratch_shapes=[pltpu.VMEM((GROUP_DIM // LANES, ts, LANES), F32)],
        compiler_params=pltpu.CompilerParams(dimension_semantics=("parallel", "parallel"),
                                             vmem_limit_bytes=VMEM_LIMIT),
        name="attn_in",
    )(x, ng.reshape(1, -1), w_in.astype(BF16), lane_gain(q_gain), lane_gain(k_gain), cos_t, sin_t, hsum)


def _band_attn_kernel(q_ref, k_ref, v_ref, bias_ref, o_ref, lse_ref, kf_ref, vf_ref):
    nres, tq = q_ref.shape[1], q_ref.shape[2]
    i = pl.program_id(2)

    @pl.when(i == 0)
    def _():
        kf_ref[:, 0:BAND, :] = jnp.zeros((nres, BAND, GROUP_DIM), BF16)
        vf_ref[:, 0:BAND, :] = jnp.zeros((nres, BAND, GROUP_DIM), BF16)

    kf_ref[:, BAND:BAND + tq, :] = k_ref[0]
    vf_ref[:, BAND:BAND + tq, :] = v_ref[0]

    lane = lax.broadcasted_iota(jnp.int32, (1, LANES), 1)
    low = lane < HEAD_DIM
    head_mask = (low.astype(BF16), (~low).astype(BF16))
    eye = (lax.broadcasted_iota(jnp.int32, (Q_BLOCK, Q_BLOCK), 0)
           == lax.broadcasted_iota(jnp.int32, (Q_BLOCK, Q_BLOCK), 1)).astype(BF16)
    ones = jnp.ones((2 * BAND, LANES), BF16)

    npair = GROUP_DIM // LANES
    lse_lanes = LANES // npair

    units = [(res, sb * Q_BLOCK, pair) for res in range(nres) for sb in range(tq // Q_BLOCK)
             for pair in range(npair)]

    scores = []
    for res, r0, pair in units:
        cols = slice(pair * LANES, (pair + 1) * LANES)
        qp = q_ref[0, res, r0:r0 + Q_BLOCK, cols]
        kp = kf_ref[res, r0:r0 + 2 * BAND, cols]
        bias_t = bias_ref[(i == 0).astype(jnp.int32)] if r0 == 0 else bias_ref[0]
        q_aug = jnp.concatenate([jnp.concatenate([qp * head_mask[hh], eye], axis=1) for hh in range(2)],
                                axis=0)
        k_aug = jnp.concatenate([kp, bias_t], axis=1)
        scores.append(lax.dot_general(q_aug, k_aug, (((1,), (1,)), ((), ())), preferred_element_type=F32))

    probs, maxes = [], []
    for s in scores:
        m = jnp.max(s, axis=-1, keepdims=True)
        probs.append(jnp.exp2(s - m).astype(BF16))
        maxes.append(m)

    lse_c = None
    for (res, r0, pair), e, m in zip(units, probs, maxes):
        cols = slice(pair * LANES, (pair + 1) * LANES)
        vp = vf_ref[res, r0:r0 + 2 * BAND, cols]
        out = _dot(e, jnp.concatenate([vp, ones], axis=1))
        acc = jnp.where(low, out[:Q_BLOCK, :LANES], out[Q_BLOCK:, :LANES])
        den = jnp.where(low, out[:Q_BLOCK, LANES:], out[Q_BLOCK:, LANES:])
        mx = jnp.where(low, m[:Q_BLOCK], m[Q_BLOCK:])
        o_ref[0, res, r0:r0 + Q_BLOCK, cols] = (acc / den).astype(BF16)
        lse = (mx + jnp.log2(den)) * LN2
        shift = (lse_lanes * pair + lse_lanes // 2 - HEAD_DIM) % LANES
        rolled = pltpu.roll(lse, shift, 1)
        lse_c = rolled if pair == 0 else jnp.where(lane // lse_lanes == pair, rolled, lse_c)
        if pair == npair - 1:
            lse_ref[0, res, r0:r0 + Q_BLOCK, :] = lse_c

    kf_ref[:, 0:BAND, :] = kf_ref[:, tq:tq + BAND, :]
    vf_ref[:, 0:BAND, :] = vf_ref[:, tq:tq + BAND, :]


def _band_bias():
    kj = jnp.arange(2 * BAND)[:, None]
    qi = jnp.arange(Q_BLOCK)[None, :]
    dist = BAND + qi - kj
    band = (dist >= 0) & (dist <= BAND)
    both = jnp.stack([band, band & (kj >= BAND)])
    return jnp.where(both, 0.0, NEG_INF).astype(BF16)


def _band_attn(q, k, v):
    B, d, L, _ = q.shape
    tq = min(BAND_ROWS, L)
    nres = min(d, BAND_ROWS // tq)
    spec = pl.BlockSpec((1, nres, tq, GROUP_DIM), lambda b, r, i: (b, r, i, 0))
    lse_spec = pl.BlockSpec((1, nres, tq, LANES), lambda b, r, i: (b, r, i, 0))
    return pl.pallas_call(
        _band_attn_kernel,
        grid=(B, d // nres, L // tq),
        in_specs=[spec, spec, spec, _const_spec((2, 2 * BAND, Q_BLOCK))],
        out_specs=[spec, lse_spec],
        out_shape=[jax.ShapeDtypeStruct(q.shape, BF16), jax.ShapeDtypeStruct((B, d, L, LANES), F32)],
        scratch_shapes=[pltpu.VMEM((nres, BAND + tq, GROUP_DIM), BF16)] * 2,
        compiler_params=pltpu.CompilerParams(dimension_semantics=("parallel", "parallel", "arbitrary"),
                                             vmem_limit_bytes=VMEM_LIMIT),
        name=f"band_attn_d{d}",
    )(q, k, v, _band_bias())


def _attn_out_kernel(x_ref, p_ref, sg_ref, o0_ref, o1_ref, o2_ref, l0_ref, l1_ref, l2_ref,
                     expand_ref, wout_ref, png_ref, wgate_ref, wproj_ref, out_ref, perm_ref):
    ts = x_ref.shape[1]

    def natural(ref, g, slab0):
        d = DILATION_GROUPS[g][1]
        nblk = ref.shape[-1] // LANES
        if d == 1:
            return ref[0].astype(F32)
        for j in range(nblk):
            for r in range(d):
                perm_ref[slab0 + j, pl.ds(r, ts // d, stride=d), :] = (
                    ref[0, r, :, j * LANES:(j + 1) * LANES].astype(F32))
        return jnp.concatenate([perm_ref[slab0 + j] for j in range(nblk)], axis=-1)

    o_refs = (o0_ref, o1_ref, o2_ref)
    l_refs = (l0_ref, l1_ref, l2_ref)
    o_slabs = GROUP_DIM // LANES
    lses = [natural(l_refs[g], g, N_GROUPS * o_slabs + g) for g in range(N_GROUPS)]
    m = jnp.maximum(jnp.maximum(lses[0], lses[1]), lses[2])
    wts = [jnp.exp(l - m) for l in lses]
    inv = 1.0 / (wts[0] + wts[1] + wts[2])
    o = sum(_dot((wts[g] * inv).astype(BF16), expand_ref[...]) * natural(o_refs[g], g, g * o_slabs)
            for g in range(N_GROUPS))
    y = (o * sg_ref[0].astype(F32)).astype(BF16)
    x1 = x_ref[0] + _dot(y, wout_ref[...])
    out_ref[0] = _ple_update(x1, p_ref[0], png_ref[...], wgate_ref[...], wproj_ref[...])


def _attn_out(x, p, layer, sg, os_, lses, w_out, png, w_gate, w_proj):
    B, S, D = x.shape
    ts = OUT_TILE

    def group_spec(g, width):
        d = DILATION_GROUPS[g][1]
        if d == 1:
            return pl.BlockSpec((1, ts, width), lambda b, s: (b, s, 0))
        return pl.BlockSpec((1, d, ts // d, width), lambda b, s: (b, 0, s, 0))

    lanes_per_head = LANES // HEADS_PER_GROUP
    expand = (jnp.arange(LANES)[:, None]
              == (jnp.arange(GROUP_DIM)[None, :] // HEAD_DIM) * lanes_per_head + lanes_per_head // 2).astype(BF16)
    return pl.pallas_call(
        _attn_out_kernel,
        grid=(B, S // ts),
        in_specs=[pl.BlockSpec((1, ts, D), lambda b, s: (b, s, 0)),
                  pl.BlockSpec((None, 1, ts, PLE_DIM), lambda b, s: (layer, b, s, 0)),
                  pl.BlockSpec((1, ts, GROUP_DIM), lambda b, s: (b, s, 0))]
        + [group_spec(g, GROUP_DIM) for g in range(N_GROUPS)]
        + [group_spec(g, LANES) for g in range(N_GROUPS)]
        + [_const_spec((LANES, GROUP_DIM)), _const_spec((GROUP_DIM, D)), _const_spec((1, D)),
           _const_spec((D, D)), _const_spec((PLE_DIM, D))],
        out_specs=pl.BlockSpec((1, ts, D), lambda b, s: (b, s, 0)),
        out_shape=jax.ShapeDtypeStruct((B, S, D), F32),
        scratch_shapes=[pltpu.VMEM((N_GROUPS * (GROUP_DIM // LANES + 1), ts, LANES), F32)],
        compiler_params=pltpu.CompilerParams(dimension_semantics=("parallel", "parallel"),
                                             vmem_limit_bytes=VMEM_LIMIT),
        name="attn_out",
    )(x, p, sg, *os_, *lses, expand, w_out.astype(BF16), png.reshape(1, -1), w_gate.astype(BF16),
      w_proj.astype(BF16))


def _attn_layer(x, p, layer, ng, w_in, q_gain, k_gain, w_out, png, w_gate, w_proj, cos_t, sin_t):
    B, S, _ = x.shape
    q0, q1, q2, k0, k1, k2, v0, v1, v2, sg = _attn_in(x, ng, w_in, q_gain, k_gain, cos_t, sin_t)
    os_, lses = [], []
    for q, k, v in ((q0, k0, v0), (q1, k1, v1), (q2, k2, v2)):
        if q.ndim == 3:
            shape = (B, 1, S, GROUP_DIM)
            o, lse = _band_attn(q.reshape(shape), k.reshape(shape), v.reshape(shape))
            o, lse = o.reshape(B, S, GROUP_DIM), lse.reshape(B, S, LANES)
        else:
            o, lse = _band_attn(q, k, v)
        os_.append(o)
        lses.append(lse)
    return _attn_out(x, p, layer, sg, os_, lses, w_out, png, w_gate, w_proj)


def kernel(x, p, positions, norm_g, conv_w_in, conv_dw, conv_dw_b, conv_ln_g, conv_ln_b, conv_w_out,
           attn_w_in, attn_q_norm, attn_k_norm, attn_w_out, ple_w_proj, ple_norm_g, ple_w_gate):
    depth = norm_g.shape[0]
    cos_t, sin_t = _rope_tables(positions)
    for i in range(depth):
        j = i // 2
        if i % 2 == 0:
            x = _conv_layer(x, p, i, norm_g[i], conv_w_in[j], conv_dw[j], conv_dw_b[j], conv_ln_g[j],
                            conv_ln_b[j], conv_w_out[j], ple_norm_g[i], ple_w_gate[i], ple_w_proj[i])
        else:
            x = _attn_layer(x, p, i, norm_g[i], attn_w_in[j], attn_q_norm[j], attn_k_norm[j],
                            attn_w_out[j], ple_norm_g[i], ple_w_gate[i], ple_w_proj[i], cos_t, sin_t)
    return x
```

```python
import functools

import jax
import jax.numpy as jnp
from jax import lax
from jax.experimental import pallas as pl
from jax.experimental.pallas import tpu as pltpu

F32 = jnp.float32
BF16 = jnp.bfloat16

D_MODEL = 1024
PLE_DIM = 256
CONV_DIM = D_MODEL
CONV_WIDTH = 31
HEAD_DIM = 64
HEADS_PER_GROUP = 8
DILATION_GROUPS = ((128, 1), (512, 4), (2048, 16))
N_GROUPS = len(DILATION_GROUPS)
GROUP_DIM = HEADS_PER_GROUP * HEAD_DIM
QKV_DIM = N_GROUPS * GROUP_DIM
BAND = 128
ROPE_THETA = 500000.0
ROT_DIM = HEAD_DIM // 4
EPS = 1e-6
NEG_INF = -1e30

LANES = 128
MXU_COLS = 256
LOG2E = 1.4426950408889634
LN2 = 0.6931471805599453
Q_SCALE = HEAD_DIM ** -0.5 * LOG2E
CONV_HALO = 32
VMEM_LIMIT = 56 * 1024 * 1024

SEQ_TILE = 1024
OUT_TILE = 1024
CONV_ROWS = 128
PROJ_ROWS = 512
TAIL_ROWS = 256
Q_BLOCK = 128
BAND_ROWS = 2048


def _const_spec(shape):
    return pl.BlockSpec(shape, lambda *_: (0,) * len(shape), pipeline_mode=pl.Buffered(1))


def _dot(a, b):
    return jnp.dot(a, b, preferred_element_type=F32)


def _rms_norm(x, g):
    return x * lax.rsqrt(jnp.mean(x * x, axis=-1, keepdims=True) + EPS) * g


def _silu(x):
    return x * jax.nn.sigmoid(x)


def _ple_update(x1, p, png, w_gate, w_proj):
    gate = jax.nn.sigmoid(_dot(_rms_norm(x1, png).astype(BF16), w_gate))
    return x1 + _dot(p.astype(BF16), w_proj) * gate


def _rope_table_kernel(pos_ref, freq_ref, sign_ref, cos_ref, sin_ref):
    ang = pos_ref[0].astype(F32) * freq_ref[...]
    cos_ref[0] = jnp.cos(ang)
    sin_ref[0] = jnp.sin(ang) * sign_ref[...]


def _rope_tables(positions):
    B, S = positions.shape
    lane = jnp.arange(LANES) % HEAD_DIM
    half = ROT_DIM // 2
    inv_freq = 1.0 / (ROPE_THETA ** (jnp.arange(0, ROT_DIM, 2, dtype=F32) / ROT_DIM))
    freq = jnp.where(lane < ROT_DIM, inv_freq[lane % half], 0.0).astype(F32)[None, :]
    sign = jnp.where(lane < half, -1.0, jnp.where(lane < ROT_DIM, 1.0, 0.0)).astype(F32)[None, :]
    ts = SEQ_TILE
    return pl.pallas_call(
        _rope_table_kernel,
        grid=(B, S // ts),
        in_specs=[pl.BlockSpec((1, ts, 1), lambda b, s: (b, s, 0)),
                  _const_spec((1, LANES)), _const_spec((1, LANES))],
        out_specs=[pl.BlockSpec((1, ts, LANES), lambda b, s: (b, s, 0))] * 2,
        out_shape=[jax.ShapeDtypeStruct((B, S, LANES), F32)] * 2,
        compiler_params=pltpu.CompilerParams(dimension_semantics=("parallel", "parallel")),
        name="rope_tables",
    )(positions.reshape(B, S, 1), freq, sign)


def _conv_layer_kernel(x_ref, p_ref, ng_ref, win_ref, dw_ref, dwb_ref, lng_ref, lnb_ref, wout_ref,
                       png_ref, wgate_ref, wproj_ref, o_ref, ypad_ref, acc_ref, gate_ref):
    ts = x_ref.shape[1]
    c = CONV_DIM
    ncb = c // LANES
    half = CONV_ROWS // 2

    @pl.when(pl.program_id(1) == 0)
    def _():
        ypad_ref[:, 0:CONV_HALO, :] = jnp.zeros((ncb, CONV_HALO, LANES), F32)

    tap0 = CONV_HALO - (CONV_WIDTH - 1)

    for p0 in range(0, ts, PROJ_ROWS):
        prow = slice(p0, p0 + PROJ_ROWS)
        h = _rms_norm(x_ref[0, prow, :], ng_ref[...]).astype(BF16)
        y = _dot(h, win_ref[:, 0:c]) * jax.nn.sigmoid(_dot(h, win_ref[:, c:2 * c]))
        for cb in range(ncb):
            ypad_ref[cb, CONV_HALO + p0:CONV_HALO + p0 + PROJ_ROWS, :] = y[:, cb * LANES:(cb + 1) * LANES]
        gate_ref[prow, :] = _silu(_dot(h, win_ref[:, 2 * c:3 * c]))

    for t0 in range(0, ts, TAIL_ROWS):
        rows = slice(t0, t0 + TAIL_ROWS)
        for r0 in range(t0, t0 + TAIL_ROWS, CONV_ROWS):
            def slab(cb, carry, r0=r0):
                for phase in range(2):
                    acc = jnp.broadcast_to(dwb_ref[cb], (half, LANES))
                    for k in range(CONV_WIDTH):
                        acc = acc + dw_ref[cb, k:k + 1, :] * ypad_ref[cb, pl.ds(r0 + phase + tap0 + k, half, stride=2), :]
                    acc_ref[cb, pl.ds(r0 + phase, half, stride=2), :] = acc
                return carry

            lax.fori_loop(0, ncb, slab, 0)
        yv = jnp.concatenate([acc_ref[cb, rows, :] for cb in range(ncb)], axis=-1)
        yc = yv - jnp.mean(yv, axis=-1, keepdims=True)
        yn = yc * lax.rsqrt(jnp.mean(yc * yc, axis=-1, keepdims=True) + EPS) * lng_ref[...] + lnb_ref[...]
        z = (_silu(yn) * gate_ref[rows, :]).astype(BF16)
        x1 = x_ref[0, rows, :] + _dot(z, wout_ref[...])
        o_ref[0, rows, :] = _ple_update(x1, p_ref[0, rows, :], png_ref[...], wgate_ref[...], wproj_ref[...])
    ypad_ref[:, 0:CONV_HALO, :] = ypad_ref[:, ts:ts + CONV_HALO, :]


def _conv_layer(x, p, layer, ng, w_in, dw, dw_b, ln_g, ln_b, w_out, png, w_gate, w_proj):
    B, S, D = x.shape
    ts = SEQ_TILE
    row = lambda v: v.reshape(1, -1).astype(F32)
    return pl.pallas_call(
        _conv_layer_kernel,
        grid=(B, S // ts),
        in_specs=[pl.BlockSpec((1, ts, D), lambda b, s: (b, s, 0)),
                  pl.BlockSpec((None, 1, ts, PLE_DIM), lambda b, s: (layer, b, s, 0)),
                  _const_spec((1, D)), _const_spec((D, 3 * CONV_DIM)),
                  _const_spec((CONV_DIM // LANES, CONV_WIDTH, LANES)), _const_spec((CONV_DIM // LANES, 1, LANES)),
                  _const_spec((1, CONV_DIM)), _const_spec((1, CONV_DIM)),
                  _const_spec((CONV_DIM, D)), _const_spec((1, D)),
                  _const_spec((D, D)), _const_spec((PLE_DIM, D))],
        out_specs=pl.BlockSpec((1, ts, D), lambda b, s: (b, s, 0)),
        out_shape=jax.ShapeDtypeStruct((B, S, D), F32),
        scratch_shapes=[pltpu.VMEM((CONV_DIM // LANES, CONV_HALO + ts, LANES), F32),
                        pltpu.VMEM((CONV_DIM // LANES, ts, LANES), F32),
                        pltpu.VMEM((ts, CONV_DIM), F32)],
        compiler_params=pltpu.CompilerParams(dimension_semantics=("parallel", "arbitrary"),
                                             vmem_limit_bytes=VMEM_LIMIT),
        name="conv_layer",
    )(x, p, row(ng), w_in.astype(BF16),
      dw.astype(F32).reshape(CONV_WIDTH, CONV_DIM // LANES, LANES).transpose(1, 0, 2),
      dw_b.astype(F32).reshape(CONV_DIM // LANES, 1, LANES), row(ln_g), row(ln_b),
      w_out.astype(BF16), row(png), w_gate.astype(BF16), w_proj.astype(BF16))


def _attn_in_kernel(x_ref, ng_ref, win_ref, qg_ref, kg_ref, cos_ref, sin_ref, hsum_ref,
                    q0_ref, q1_ref, q2_ref, k0_ref, k1_ref, k2_ref, v0_ref, v1_ref, v2_ref,
                    sg_ref, perm_ref):
    ts = x_ref.shape[1]
    h = _rms_norm(x_ref[0], ng_ref[...]).astype(BF16)
    cos_t = cos_ref[0]
    sin_t = sin_ref[0]
    lane = lax.broadcasted_iota(jnp.int32, (1, LANES), 1) % HEAD_DIM
    first_half = lane < (ROT_DIM // 2)
    outs = ((q0_ref, q1_ref, q2_ref), (k0_ref, k1_ref, k2_ref), (v0_ref, v1_ref, v2_ref))
    gains = (qg_ref, kg_ref)

    def emit(out_ref, g, j, val, slab):
        d = DILATION_GROUPS[g][1]
        cols = slice(j * LANES, (j + 1) * LANES)
        if d == 1:
            out_ref[0, :, cols] = val.astype(BF16)
            return
        perm_ref[slab] = val
        for r in range(d):
            out_ref[0, r, :, cols] = perm_ref[slab, pl.ds(r, ts // d, stride=d), :].astype(BF16)

    for kind in range(3):
        for g in range(N_GROUPS):
            c0 = kind * QKV_DIM + g * GROUP_DIM
            u = _dot(h, win_ref[:, c0:c0 + GROUP_DIM])
            if kind < 2:
                ss = jnp.concatenate(
                    [_dot((uw * uw).astype(BF16), hsum_ref[...])
                     for uw in (u[:, :MXU_COLS], u[:, MXU_COLS:])], axis=-1)
                u = u * lax.rsqrt(ss * (1.0 / HEAD_DIM) + EPS)
            for j in range(GROUP_DIM // LANES):
                ub = u[:, j * LANES:(j + 1) * LANES]
                if kind < 2:
                    un = ub * gains[kind][...]
                    partner = jnp.where(first_half, pltpu.roll(un, LANES - ROT_DIM // 2, 1),
                                        pltpu.roll(un, ROT_DIM // 2, 1))
                    ub = un * cos_t + partner * sin_t
                    if kind == 0:
                        ub = ub * Q_SCALE
                emit(outs[kind][g], g, j, ub, j)
    gate = _dot(h, win_ref[:, 3 * QKV_DIM:3 * QKV_DIM + GROUP_DIM])
    sg_ref[0] = _silu(gate).astype(BF16)


def _attn_in(x, ng, w_in, q_gain, k_gain, cos_t, sin_t):
    B, S, D = x.shape
    ts = SEQ_TILE
    hsum = (jnp.arange(MXU_COLS)[:, None] // HEAD_DIM == jnp.arange(MXU_COLS)[None, :] // HEAD_DIM).astype(BF16)
    lane_gain = lambda v: jnp.tile(v.astype(F32), LANES // HEAD_DIM)[None, :]

    def group_spec(g):
        d = DILATION_GROUPS[g][1]
        if d == 1:
            return pl.BlockSpec((1, ts, GROUP_DIM), lambda b, s: (b, s, 0))
        return pl.BlockSpec((1, d, ts // d, GROUP_DIM), lambda b, s: (b, 0, s, 0))

    def group_shape(g):
        d = DILATION_GROUPS[g][1]
        if d == 1:
            return jax.ShapeDtypeStruct((B, S, GROUP_DIM), BF16)
        return jax.ShapeDtypeStruct((B, d, S // d, GROUP_DIM), BF16)

    w_cols = 3 * QKV_DIM + GROUP_DIM
    return pl.pallas_call(
        _attn_in_kernel,
        grid=(B, S // ts),
        in_specs=[pl.BlockSpec((1, ts, D), lambda b, s: (b, s, 0)),
                  _const_spec((1, D)), _const_spec((D, w_cols)),
                  _const_spec((1, LANES)), _const_spec((1, LANES)),
                  pl.BlockSpec((1, ts, LANES), lambda b, s: (b, s, 0)),
                  pl.BlockSpec((1, ts, LANES), lambda b, s: (b, s, 0)),
                  _const_spec((MXU_COLS, MXU_COLS))],
        out_specs=[group_spec(g) for _ in range(3) for g in range(N_GROUPS)]
        + [pl.BlockSpec((1, ts, GROUP_DIM), lambda b, s: (b, s, 0))],
        out_shape=[group_shape(g) for _ in range(3) for g in range(N_GROUPS)]
        + [jax.ShapeDtypeStruct((B, S, GROUP_DIM), BF16)],
        scratch_shapes=[pltpu.VMEM((GROUP_DIM // LANES, ts, LANES), F32)],
        compiler_params=pltpu.CompilerParams(dimension_semantics=("parallel", "parallel"),
                                             vmem_limit_bytes=VMEM_LIMIT),
        name="attn_in",
    )(x, ng.reshape(1, -1), w_in.astype(BF16), lane_gain(q_gain), lane_gain(k_gain), cos_t, sin_t, hsum)


def _band_attn_kernel(q_ref, k_ref, v_ref, bias_ref, o_ref, lse_ref, kf_ref, vf_ref):
    nres, tq = q_ref.shape[1], q_ref.shape[2]
    i = pl.program_id(2)

    @pl.when(i == 0)
    def _():
        kf_ref[:, 0:BAND, :] = jnp.zeros((nres, BAND, GROUP_DIM), BF16)
        vf_ref[:, 0:BAND, :] = jnp.zeros((nres, BAND, GROUP_DIM), BF16)

    kf_ref[:, BAND:BAND + tq, :] = k_ref[0]
    vf_ref[:, BAND:BAND + tq, :] = v_ref[0]

    lane = lax.broadcasted_iota(jnp.int32, (1, LANES), 1)
    low = lane < HEAD_DIM
    head_mask = (low.astype(BF16), (~low).astype(BF16))
    eye = (lax.broadcasted_iota(jnp.int32, (Q_BLOCK, Q_BLOCK), 0)
           == lax.broadcasted_iota(jnp.int32, (Q_BLOCK, Q_BLOCK), 1)).astype(BF16)
    ones = jnp.ones((2 * BAND, LANES), BF16)

    npair = GROUP_DIM // LANES
    lse_lanes = LANES // npair

    units = [(res, sb * Q_BLOCK, pair) for res in range(nres) for sb in range(tq // Q_BLOCK)
             for pair in range(npair)]

    scores = []
    for res, r0, pair in units:
        cols = slice(pair * LANES, (pair + 1) * LANES)
        qp = q_ref[0, res, r0:r0 + Q_BLOCK, cols]
        kp = kf_ref[res, r0:r0 + 2 * BAND, cols]
        bias_t = bias_ref[(i == 0).astype(jnp.int32)] if r0 == 0 else bias_ref[0]
        q_aug = jnp.concatenate([jnp.concatenate([qp * head_mask[hh], eye], axis=1) for hh in range(2)],
                                axis=0)
        k_aug = jnp.concatenate([kp, bias_t], axis=1)
        scores.append(lax.dot_general(q_aug, k_aug, (((1,), (1,)), ((), ())), preferred_element_type=F32))

    probs, maxes = [], []
    for s in scores:
        m = jnp.max(s, axis=-1, keepdims=True)
        probs.append(jnp.exp2(s - m).astype(BF16))
        maxes.append(m)

    lse_c = None
    for (res, r0, pair), e, m in zip(units, probs, maxes):
        cols = slice(pair * LANES, (pair + 1) * LANES)
        vp = vf_ref[res, r0:r0 + 2 * BAND, cols]
        out = _dot(e, jnp.concatenate([vp, ones], axis=1))
        acc = jnp.where(low, out[:Q_BLOCK, :LANES], out[Q_BLOCK:, :LANES])
        den = jnp.where(low, out[:Q_BLOCK, LANES:], out[Q_BLOCK:, LANES:])
        mx = jnp.where(low, m[:Q_BLOCK], m[Q_BLOCK:])
        o_ref[0, res, r0:r0 + Q_BLOCK, cols] = (acc / den).astype(BF16)
        lse = (mx + jnp.log2(den)) * LN2
        shift = (lse_lanes * pair + lse_lanes // 2 - HEAD_DIM) % LANES
        rolled = pltpu.roll(lse, shift, 1)
        lse_c = rolled if pair == 0 else jnp.where(lane // lse_lanes == pair, rolled, lse_c)
        if pair == npair - 1:
            lse_ref[0, res, r0:r0 + Q_BLOCK, :] = lse_c

    kf_ref[:, 0:BAND, :] = kf_ref[:, tq:tq + BAND, :]
    vf_ref[:, 0:BAND, :] = vf_ref[:, tq:tq + BAND, :]


def _band_bias():
    kj = jnp.arange(2 * BAND)[:, None]
    qi = jnp.arange(Q_BLOCK)[None, :]
    dist = BAND + qi - kj
    band = (dist >= 0) & (dist <= BAND)
    both = jnp.stack([band, band & (kj >= BAND)])
    return jnp.where(both, 0.0, NEG_INF).astype(BF16)


def _band_attn(q, k, v):
    B, d, L, _ = q.shape
    tq = min(BAND_ROWS, L)
    nres = min(d, BAND_ROWS // tq)
    spec = pl.BlockSpec((1, nres, tq, GROUP_DIM), lambda b, r, i: (b, r, i, 0))
    lse_spec = pl.BlockSpec((1, nres, tq, LANES), lambda b, r, i: (b, r, i, 0))
    return pl.pallas_call(
        _band_attn_kernel,
        grid=(B, d // nres, L // tq),
        in_specs=[spec, spec, spec, _const_spec((2, 2 * BAND, Q_BLOCK))],
        out_specs=[spec, lse_spec],
        out_shape=[jax.ShapeDtypeStruct(q.shape, BF16), jax.ShapeDtypeStruct((B, d, L, LANES), F32)],
        scratch_shapes=[pltpu.VMEM((nres, BAND + tq, GROUP_DIM), BF16)] * 2,
        compiler_params=pltpu.CompilerParams(dimension_semantics=("parallel", "parallel", "arbitrary"),
                                             vmem_limit_bytes=VMEM_LIMIT),
        name=f"band_attn_d{d}",
    )(q, k, v, _band_bias())


def _attn_out_kernel(x_ref, p_ref, sg_ref, o0_ref, o1_ref, o2_ref, l0_ref, l1_ref, l2_ref,
                     expand_ref, wout_ref, png_ref, wgate_ref, wproj_ref, out_ref, perm_ref):
    ts = x_ref.shape[1]

    def natural(ref, g, slab0):
        d = DILATION_GROUPS[g][1]
        nblk = ref.shape[-1] // LANES
        if d == 1:
            return ref[0].astype(F32)
        for j in range(nblk):
            for r in range(d):
                perm_ref[slab0 + j, pl.ds(r, ts // d, stride=d), :] = (
                    ref[0, r, :, j * LANES:(j + 1) * LANES].astype(F32))
        return jnp.concatenate([perm_ref[slab0 + j] for j in range(nblk)], axis=-1)

    o_refs = (o0_ref, o1_ref, o2_ref)
    l_refs = (l0_ref, l1_ref, l2_ref)
    o_slabs = GROUP_DIM // LANES
    lses = [natural(l_refs[g], g, N_GROUPS * o_slabs + g) for g in range(N_GROUPS)]
    m = jnp.maximum(jnp.maximum(lses[0], lses[1]), lses[2])
    wts = [jnp.exp(l - m) for l in lses]
    inv = 1.0 / (wts[0] + wts[1] + wts[2])
    o = sum(_dot((wts[g] * inv).astype(BF16), expand_ref[...]) * natural(o_refs[g], g, g * o_slabs)
            for g in range(N_GROUPS))
    y = (o * sg_ref[0].astype(F32)).astype(BF16)
    x1 = x_ref[0] + _dot(y, wout_ref[...])
    out_ref[0] = _ple_update(x1, p_ref[0], png_ref[...], wgate_ref[...], wproj_ref[...])


def _attn_out(x, p, layer, sg, os_, lses, w_out, png, w_gate, w_proj):
    B, S, D = x.shape
    ts = OUT_TILE

    def group_spec(g, width):
        d = DILATION_GROUPS[g][1]
        if d == 1:
            return pl.BlockSpec((1, ts, width), lambda b, s: (b, s, 0))
        return pl.BlockSpec((1, d, ts // d, width), lambda b, s: (b, 0, s, 0))

    lanes_per_head = LANES // HEADS_PER_GROUP
    expand = (jnp.arange(LANES)[:, None]
              == (jnp.arange(GROUP_DIM)[None, :] // HEAD_DIM) * lanes_per_head + lanes_per_head // 2).astype(BF16)
    return pl.pallas_call(
        _attn_out_kernel,
        grid=(B, S // ts),
        in_specs=[pl.BlockSpec((1, ts, D), lambda b, s: (b, s, 0)),
                  pl.BlockSpec((None, 1, ts, PLE_DIM), lambda b, s: (layer, b, s, 0)),
                  pl.BlockSpec((1, ts, GROUP_DIM), lambda b, s: (b, s, 0))]
        + [group_spec(g, GROUP_DIM) for g in range(N_GROUPS)]
        + [group_spec(g, LANES) for g in range(N_GROUPS)]
        + [_const_spec((LANES, GROUP_DIM)), _const_spec((GROUP_DIM, D)), _const_spec((1, D)),
           _const_spec((D, D)), _const_spec((PLE_DIM, D))],
        out_specs=pl.BlockSpec((1, ts, D), lambda b, s: (b, s, 0)),
        out_shape=jax.ShapeDtypeStruct((B, S, D), F32),
        scratch_shapes=[pltpu.VMEM((N_GROUPS * (GROUP_DIM // LANES + 1), ts, LANES), F32)],
        compiler_params=pltpu.CompilerParams(dimension_semantics=("parallel", "parallel"),
                                             vmem_limit_bytes=VMEM_LIMIT),
        name="attn_out",
    )(x, p, sg, *os_, *lses, expand, w_out.astype(BF16), png.reshape(1, -1), w_gate.astype(BF16),
      w_proj.astype(BF16))


def _attn_layer(x, p, layer, ng, w_in, q_gain, k_gain, w_out, png, w_gate, w_proj, cos_t, sin_t):
    B, S, _ = x.shape
    q0, q1, q2, k0, k1, k2, v0, v1, v2, sg = _attn_in(x, ng, w_in, q_gain, k_gain, cos_t, sin_t)
    os_, lses = [], []
    for q, k, v in ((q0, k0, v0), (q1, k1, v1), (q2, k2, v2)):
        if q.ndim == 3:
            shape = (B, 1, S, GROUP_DIM)
            o, lse = _band_attn(q.reshape(shape), k.reshape(shape), v.reshape(shape))
            o, lse = o.reshape(B, S, GROUP_DIM), lse.reshape(B, S, LANES)
        else:
            o, lse = _band_attn(q, k, v)
        os_.append(o)
        lses.append(lse)
    return _attn_out(x, p, layer, sg, os_, lses, w_out, png, w_gate, w_proj)


def kernel(x, p, positions, norm_g, conv_w_in, conv_dw, conv_dw_b, conv_ln_g, conv_ln_b, conv_w_out,
           attn_w_in, attn_q_norm, attn_k_norm, attn_w_out, ple_w_proj, ple_norm_g, ple_w_gate):
    depth = norm_g.shape[0]
    cos_t, sin_t = _rope_tables(positions)
    for i in range(depth):
        j = i // 2
        if i % 2 == 0:
            x = _conv_layer(x, p, i, norm_g[i], conv_w_in[j], conv_dw[j], conv_dw_b[j], conv_ln_g[j],
                            conv_ln_b[j], conv_w_out[j], ple_norm_g[i], ple_w_gate[i], ple_w_proj[i])
        else:
            x = _attn_layer(x, p, i, norm_g[i], attn_w_in[j], attn_q_norm[j], attn_k_norm[j],
                            attn_w_out[j], ple_norm_g[i], ple_w_gate[i], ple_w_proj[i], cos_t, sin_t)
    return x
```
